```python
import math
import jax, jax.numpy as jnp
from jax import lax
import numpy as np

D_MODEL = 1024
BATCH = 8
SEQ = 2048
DEPTH = 4
DEC_BATCH = 128
DEC_SEQ = 8
PAST_LEN = 2048
PAGE_SIZE = 128

MIX_W = D_MODEL
N_GROUPS = 4
GROUP_W = MIX_W // N_GROUPS
GROUP_HEADS = 4
HEAD_DIM = GROUP_W // GROUP_HEADS
PROJ_W = 10 * GROUP_W
CHUNK = 128
LRU_C = 8.0
LRU_CONV = 4
SCONV = 3
SB_BLOCK = 128
SB_BIAS_INIT = -7.0
N_MEM = 256
MEM_HEADS = 4
MEM_HEAD_DIM = D_MODEL // 8
MEM_W = MEM_HEADS * MEM_HEAD_DIM
N_EXPERTS = 32
TOP_K = 4
D_EXPERT = D_MODEL
SWIGLU_LIMIT = 7.0
SWIGLU_ALPHA = 1.702
MOE_BLOCK = 128
RMS_EPS = 1e-6

kernel_name = 'hybrid_gmlp_rglru_stickbreak_shortconv_moe_step'


def rms_norm(x, g):
    xf = x.astype(jnp.float32)
    y = xf * lax.rsqrt(jnp.mean(xf * xf, axis=-1, keepdims=True) + RMS_EPS)
    return (y * g.astype(jnp.float32)).astype(x.dtype)


def causal_dwconv(x, buf, w):
    width = w.shape[0]
    T = x.shape[1]
    xp = jnp.concatenate([buf.astype(x.dtype), x], axis=1)
    y = xp[:, 0:T] * w[0]
    for k in range(1, width):
        y = y + xp[:, k:k + T] * w[k]
    return y, xp[:, T:]


def gmlp_chunk_mix(v, ws, bs):
    B, T, H, d = v.shape
    L = min(CHUNK, T)
    nc = T // L
    vc = v.reshape(B, nc, L, H, d)
    w = jnp.tril(ws[:, :L, :L])
    mixed = jnp.einsum('hts,bcshd->bcthd', w, vc) + jnp.transpose(bs[:, :L])[None, None, :, :, None]
    return mixed.reshape(B, T, H, d), vc[:, nc - 1]


def _lin_combine(earlier, later):
    a1, b1 = earlier
    a2, b2 = later
    return a1 * a2, a2 * b1 + b2


def rg_lru(x, h0, wa, ba, wx, bx, lam):
    B, T, W = x.shape
    xf = x.astype(jnp.float32)
    xh = xf.reshape(B, T, GROUP_HEADS, HEAD_DIM)
    r = jax.nn.sigmoid(jnp.einsum('bthi,hij->bthj', xh, wa.astype(jnp.float32)) + ba.astype(jnp.float32)).reshape(B, T, W)
    ig = jax.nn.sigmoid(jnp.einsum('bthi,hij->bthj', xh, wx.astype(jnp.float32)) + bx.astype(jnp.float32)).reshape(B, T, W)
    log_a = LRU_C * r * jax.nn.log_sigmoid(lam.astype(jnp.float32))
    a = jnp.exp(log_a)
    b = jnp.sqrt(-jnp.expm1(2.0 * log_a)) * (ig * xf)
    a_cum, b_cum = lax.associative_scan(_lin_combine, (a, b), axis=1)
    h = a_cum * h0.astype(jnp.float32)[:, None] + b_cum
    return h, h[:, -1].astype(h0.dtype)


def stick_breaking_attention(q, k, v, bias, q_pos0):
    B, Tq, H, d = q.shape
    Tk = k.shape[1]
    qb = Tq if Tq <= SB_BLOCK else SB_BLOCK
    nb = Tq // qb
    q_blocks = jnp.moveaxis(q.reshape(B, nb, qb, H, d), 1, 0)
    starts = q_pos0 + jnp.arange(nb, dtype=jnp.int32) * qb
    k_pos = jnp.arange(Tk, dtype=jnp.int32)
    kf = k.astype(jnp.float32)
    vf = v.astype(jnp.float32)
    bf = bias.astype(jnp.float32)[None, :, None, None]
    scale = d ** -0.5

    def block(args):
        qblk, start = args
        q_pos = start + jnp.arange(qb, dtype=jnp.int32)
        z = jnp.einsum('bqhd,bkhd->bhqk', qblk.astype(jnp.float32), kf) * scale + bf
        mask = k_pos[None, :] < q_pos[:, None]
        log_keep = jnp.where(mask, jax.nn.log_sigmoid(-z), 0.0)
        after = lax.cumsum(log_keep, axis=3, reverse=True) - log_keep
        w = jnp.where(mask, jnp.exp(jax.nn.log_sigmoid(z) + after), 0.0)
        return jnp.einsum('bhqk,bkhd->bqhd', w, vf)

    out = lax.map(block, (q_blocks, starts))
    return jnp.moveaxis(out, 0, 1).reshape(B, Tq, H, d).astype(q.dtype)


def memory_attention(q, k, v):
    d = q.shape[-1]
    s = jnp.einsum('bthd,bnhd->bhtn', q.astype(jnp.float32), k.astype(jnp.float32)) * (d ** -0.5)
    p = jax.nn.softmax(s, axis=-1)
    return jnp.einsum('bhtn,bnhd->bthd', p, v.astype(jnp.float32)).astype(q.dtype)


def moe_ffn(x, router_w, router_b, w_in, b_in, w_out, b_out):
    B, T, Dm = x.shape
    n = B * T
    xf = x.reshape(n, Dm)
    logits = xf.astype(jnp.float32) @ router_w.astype(jnp.float32) + router_b.astype(jnp.float32)
    top_val, top_idx = lax.top_k(logits, TOP_K)
    gates = jax.nn.softmax(top_val, axis=-1)
    n_assign = n * TOP_K
    flat_e = top_idx.reshape(-1).astype(jnp.int32)
    flat_tok = jnp.arange(n_assign, dtype=jnp.int32) // TOP_K
    flat_g = gates.reshape(-1)
    order = jnp.argsort(flat_e)
    sorted_e = flat_e[order]
    counts = jnp.zeros((N_EXPERTS,), jnp.int32).at[flat_e].add(1)
    starts = jnp.cumsum(counts) - counts
    padded = (counts + MOE_BLOCK - 1) // MOE_BLOCK * MOE_BLOCK
    pad_ends = jnp.cumsum(padded)
    pad_starts = pad_ends - padded
    rank = jnp.arange(n_assign, dtype=jnp.int32) - starts[sorted_e]
    dest = pad_starts[sorted_e] + rank
    n_blocks = -(-n_assign // MOE_BLOCK) + N_EXPERTS
    slots = n_blocks * MOE_BLOCK
    slot_tok = jnp.full((slots,), n, jnp.int32).at[dest].set(flat_tok[order])
    slot_gate = jnp.zeros((slots,), jnp.float32).at[dest].set(flat_g[order])
    block_start = jnp.arange(n_blocks, dtype=jnp.int32) * MOE_BLOCK
    block_e = jnp.minimum(jnp.searchsorted(pad_ends, block_start, side='right'), N_EXPERTS - 1).astype(jnp.int32)
    x_pad = jnp.concatenate([xf, jnp.zeros((1, Dm), xf.dtype)], axis=0)
    xb = x_pad[slot_tok].reshape(n_blocks, MOE_BLOCK, Dm)

    def expert_block(args):
        xblk, e = args
        hdn = xblk @ w_in[e] + b_in[e]
        g, u = hdn[:, :D_EXPERT], hdn[:, D_EXPERT:]
        g = jnp.minimum(g, SWIGLU_LIMIT)
        u = jnp.clip(u, -SWIGLU_LIMIT, SWIGLU_LIMIT)
        act = g * jax.nn.sigmoid(SWIGLU_ALPHA * g) * (u + 1.0)
        return act @ w_out[e] + b_out[e]

    yb = lax.map(expert_block, (xb, block_e)).reshape(slots, Dm)
    y = jnp.zeros((n + 1, Dm), jnp.float32).at[slot_tok].add(yb.astype(jnp.float32) * slot_gate[:, None])
    return y[:n].reshape(B, T, Dm).astype(x.dtype)


def gather_pages(pool, page_table):
    g = pool[page_table]
    B, npg, ps, H, d = g.shape
    return g.reshape(B, npg * ps, H, d)


def _normal(key, shape, scale):
    return jax.random.normal(key, shape, jnp.float32) * scale


def _gain(key, shape):
    return 1.0 + 0.02 * jax.random.normal(key, shape, jnp.float32)


def setup_inputs(seed: int = 0) -> dict:
    key = jax.random.key(seed)
    ks = jax.random.split(key, 48)
    n_pages = PAST_LEN // PAGE_SIZE
    n_used = DEC_BATCH * n_pages
    n_pool = n_used + max(n_used // 4, 1)
    page_table = jax.random.permutation(ks[0], n_pool)[:n_used].reshape(DEC_BATCH, n_pages).astype(jnp.int32)
    a0 = jax.random.uniform(ks[1], (DEPTH, GROUP_W), jnp.float32, 0.9, 0.999)
    a_base = a0 ** (1.0 / LRU_C)
    lru_lambda = jnp.log(a_base) - jnp.log1p(-a_base)
    return {
        'x_prompt': _normal(ks[2], (BATCH, SEQ, D_MODEL), 1.0),
        'x_sample': _normal(ks[3], (DEC_BATCH, DEC_SEQ, D_MODEL), 1.0),
        'cache_sb_k': _normal(ks[4], (DEPTH, n_pool, PAGE_SIZE, GROUP_HEADS, HEAD_DIM), 1.0),
        'cache_sb_v': _normal(ks[5], (DEPTH, n_pool, PAGE_SIZE, GROUP_HEADS, HEAD_DIM), 1.0),
        'page_table': page_table,
        'state_lru_h': _normal(ks[6], (DEPTH, DEC_BATCH, GROUP_W), 0.5),
        'state_lru_conv': _normal(ks[7], (DEPTH, DEC_BATCH, LRU_CONV - 1, GROUP_W), 1.0),
        'state_sconv': _normal(ks[8], (DEPTH, DEC_BATCH, SCONV - 1, GROUP_W), 1.0),
        'cache_mem_k': _normal(ks[9], (DEPTH, DEC_BATCH, N_MEM, MEM_HEADS, MEM_HEAD_DIM), 1.0),
        'cache_mem_v': _normal(ks[10], (DEPTH, DEC_BATCH, N_MEM, MEM_HEADS, MEM_HEAD_DIM), 1.0),
        'mem_prompt': _normal(ks[11], (BATCH, N_MEM, D_MODEL), 1.0),
        'ln_mix': _gain(ks[12], (DEPTH, D_MODEL)),
        'w_in': _normal(ks[13], (DEPTH, D_MODEL, PROJ_W), D_MODEL ** -0.5),
        'gmlp_norm': _gain(ks[14], (DEPTH, GROUP_W)),
        'gmlp_ws': _normal(ks[15], (DEPTH, GROUP_HEADS, CHUNK, CHUNK), 0.5 * CHUNK ** -0.5),
        'gmlp_bs': 1.0 + 0.1 * jax.random.normal(ks[16], (DEPTH, GROUP_HEADS, CHUNK), jnp.float32),
        'lru_conv_w': _normal(ks[17], (DEPTH, LRU_CONV, GROUP_W), LRU_CONV ** -0.5),
        'lru_conv_b': _normal(ks[18], (DEPTH, GROUP_W), 0.01),
        'lru_wa': _normal(ks[19], (DEPTH, GROUP_HEADS, HEAD_DIM, HEAD_DIM), HEAD_DIM ** -0.5),
        'lru_ba': _normal(ks[20], (DEPTH, GROUP_HEADS, HEAD_DIM), 0.01),
        'lru_wx': _normal(ks[21], (DEPTH, GROUP_HEADS, HEAD_DIM, HEAD_DIM), HEAD_DIM ** -0.5),
        'lru_bx': _normal(ks[22], (DEPTH, GROUP_HEADS, HEAD_DIM), 0.01),
        'lru_lambda': lru_lambda,
        'sb_q_norm': _gain(ks[23], (DEPTH, HEAD_DIM)),
        'sb_k_norm': _gain(ks[24], (DEPTH, HEAD_DIM)),
        'sb_bias': SB_BIAS_INIT + 0.3 * jax.random.normal(ks[43], (DEPTH, GROUP_HEADS), jnp.float32),
        'sconv_w': _normal(ks[25], (DEPTH, SCONV, GROUP_W), SCONV ** -0.5),
        'out_norm': _gain(ks[26], (DEPTH, N_GROUPS, GROUP_W)),
        'w_out': _normal(ks[27], (DEPTH, MIX_W, D_MODEL), 0.5 * MIX_W ** -0.5),
        'ln_mem': _gain(ks[28], (DEPTH, D_MODEL)),
        'ln_mem_kv': _gain(ks[29], (DEPTH, D_MODEL)),
        'mem_wq': _normal(ks[30], (DEPTH, D_MODEL, MEM_W), D_MODEL ** -0.5),
        'mem_wk': _normal(ks[31], (DEPTH, D_MODEL, MEM_W), D_MODEL ** -0.5),
        'mem_wv': _normal(ks[32], (DEPTH, D_MODEL, MEM_W), D_MODEL ** -0.5),
        'mem_q_norm': _gain(ks[33], (DEPTH, MEM_HEAD_DIM)),
        'mem_k_norm': _gain(ks[34], (DEPTH, MEM_HEAD_DIM)),
        'mem_wo': _normal(ks[35], (DEPTH, MEM_W, D_MODEL), 0.5 * MEM_W ** -0.5),
        'ln_moe': _gain(ks[36], (DEPTH, D_MODEL)),
        'router_w': _normal(ks[37], (DEPTH, D_MODEL, N_EXPERTS), D_MODEL ** -0.5),
        'router_b': _normal(ks[38], (DEPTH, N_EXPERTS), 0.01),
        'moe_w_in': _normal(ks[39], (DEPTH, N_EXPERTS, D_MODEL, 2 * D_EXPERT), D_MODEL ** -0.5),
        'moe_b_in': _normal(ks[40], (DEPTH, N_EXPERTS, 2 * D_EXPERT), 0.01),
        'moe_w_out': _normal(ks[41], (DEPTH, N_EXPERTS, D_EXPERT, D_MODEL), 0.5 * D_EXPERT ** -0.5),
        'moe_b_out': _normal(ks[42], (DEPTH, N_EXPERTS, D_MODEL), 0.01),
    }


def reference(x_prompt, x_sample, cache_sb_k, cache_sb_v, page_table, state_lru_h, state_lru_conv,
              state_sconv, cache_mem_k, cache_mem_v, mem_prompt, ln_mix, w_in, gmlp_norm, gmlp_ws,
              gmlp_bs, lru_conv_w, lru_conv_b, lru_wa, lru_ba, lru_wx, lru_bx, lru_lambda, sb_q_norm,
              sb_k_norm, sb_bias, sconv_w, out_norm, w_out, ln_mem, ln_mem_kv, mem_wq, mem_wk, mem_wv,
              mem_q_norm, mem_k_norm, mem_wo, ln_moe, router_w, router_b, moe_w_in, moe_b_in,
              moe_w_out, moe_b_out):

    def mixer(h, l, past_k, past_v, h0, lru_buf, sconv_buf):
        B, T, _ = h.shape
        proj = h @ w_in[l]
        a_u, a_v, b_x, b_g, c_q, c_k, c_v, d_b, d_c, d_x = jnp.split(proj, 10, axis=-1)
        a_v = rms_norm(jax.nn.gelu(a_v), gmlp_norm[l]).reshape(B, T, GROUP_HEADS, HEAD_DIM)
        a_mix, a_state = gmlp_chunk_mix(a_v, gmlp_ws[l], gmlp_bs[l])
        y_a = jax.nn.gelu(a_u) * a_mix.reshape(B, T, GROUP_W)
        b_conv, b_buf = causal_dwconv(b_x, lru_buf, lru_conv_w[l])
        b_h, b_last = rg_lru(b_conv + lru_conv_b[l], h0, lru_wa[l], lru_ba[l], lru_wx[l], lru_bx[l], lru_lambda[l])
        y_b = b_h.astype(h.dtype) * jax.nn.gelu(b_g)
        q = rms_norm(c_q.reshape(B, T, GROUP_HEADS, HEAD_DIM), sb_q_norm[l])
        k = rms_norm(c_k.reshape(B, T, GROUP_HEADS, HEAD_DIM), sb_k_norm[l])
        v = c_v.reshape(B, T, GROUP_HEADS, HEAD_DIM)
        if past_k is None:
            k_all, v_all, pos0 = k, v, 0
        else:
            k_all = jnp.concatenate([past_k.astype(k.dtype), k], axis=1)
            v_all = jnp.concatenate([past_v.astype(v.dtype), v], axis=1)
            pos0 = past_k.shape[1]
        y_c = stick_breaking_attention(q, k_all, v_all, sb_bias[l], pos0).reshape(B, T, GROUP_W)
        d_conv, d_buf = causal_dwconv(d_c * d_x, sconv_buf, sconv_w[l])
        y_d = d_b * d_conv
        y = jnp.stack([y_a, y_b, y_c, y_d], axis=2)
        y = rms_norm(y, out_norm[l]).reshape(B, T, MIX_W)
        return y @ w_out[l], (k, v, b_last, b_buf, d_buf, a_state)

    def mem_kv(mem, l):
        m = rms_norm(mem, ln_mem_kv[l])
        B, N, _ = m.shape
        mk = rms_norm((m @ mem_wk[l]).reshape(B, N, MEM_HEADS, MEM_HEAD_DIM), mem_k_norm[l])
        mv = (m @ mem_wv[l]).reshape(B, N, MEM_HEADS, MEM_HEAD_DIM)
        return mk, mv

    def layer(x, l, past_k, past_v, h0, lru_buf, sconv_buf, mk, mv):
        B, T, _ = x.shape
        y_mix, st = mixer(rms_norm(x, ln_mix[l]), l, past_k, past_v, h0, lru_buf, sconv_buf)
        x = x + y_mix
        q = rms_norm((rms_norm(x, ln_mem[l]) @ mem_wq[l]).reshape(B, T, MEM_HEADS, MEM_HEAD_DIM), mem_q_norm[l])
        x = x + memory_attention(q, mk.astype(q.dtype), mv.astype(q.dtype)).reshape(B, T, MEM_W) @ mem_wo[l]
        x = x + moe_ffn(rms_norm(x, ln_moe[l]), router_w[l], router_b[l], moe_w_in[l], moe_b_in[l],
                        moe_w_out[l], moe_b_out[l])
        return x, st

    xp = x_prompt
    xs = x_sample
    bp = x_prompt.shape[0]
    zeros_h = jnp.zeros((bp, GROUP_W), x_prompt.dtype)
    zeros_lru_buf = jnp.zeros((bp, LRU_CONV - 1, GROUP_W), x_prompt.dtype)
    zeros_sconv_buf = jnp.zeros((bp, SCONV - 1, GROUP_W), x_prompt.dtype)
    st_p = [[] for _ in range(6)]
    st_s = [[] for _ in range(6)]
    mk_list, mv_list = [], []
    for l in range(DEPTH):
        mk_p, mv_p = mem_kv(mem_prompt, l)
        mk_list.append(mk_p)
        mv_list.append(mv_p)
        xp, sp = layer(xp, l, None, None, zeros_h, zeros_lru_buf, zeros_sconv_buf, mk_p, mv_p)
        past_k = gather_pages(cache_sb_k[l], page_table)
        past_v = gather_pages(cache_sb_v[l], page_table)
        xs, ss = layer(xs, l, past_k, past_v, state_lru_h[l], state_lru_conv[l], state_sconv[l],
                       cache_mem_k[l], cache_mem_v[l])
        for i in range(6):
            st_p[i].append(sp[i])
            st_s[i].append(ss[i])

    sb_k_prompt = jnp.stack(st_p[0])
    sb_v_prompt = jnp.stack(st_p[1])
    sb_k_sample = jnp.stack(st_s[0])
    sb_v_sample = jnp.stack(st_s[1])
    lru_h_prompt = jnp.stack(st_p[2])
    lru_h_sample = jnp.stack(st_s[2])
    lru_conv_prompt = jnp.stack(st_p[3])
    lru_conv_sample = jnp.stack(st_s[3])
    sconv_prompt = jnp.stack(st_p[4])
    sconv_sample = jnp.stack(st_s[4])
    gmlp_v_prompt = jnp.stack(st_p[5])
    gmlp_v_sample = jnp.stack(st_s[5])
    mem_k_prompt = jnp.stack(mk_list)
    mem_v_prompt = jnp.stack(mv_list)
    return (xp, xs, sb_k_prompt, sb_v_prompt, sb_k_sample, sb_v_sample, lru_h_prompt, lru_h_sample,
            lru_conv_prompt, lru_conv_sample, sconv_prompt, sconv_sample, gmlp_v_prompt, gmlp_v_sample,
            mem_k_prompt, mem_v_prompt)
```

```python
import functools
import math

import jax
import jax.numpy as jnp
from jax import lax
from jax.experimental import pallas as pl
from jax.experimental.pallas import tpu as pltpu

F32 = jnp.float32
BF16 = jnp.bfloat16

D_MODEL = 1024
GROUP_W = 256
N_HEADS = 4
HEAD_DIM = 64
HEAD_SHIFT = 6
PROJ_W = 10 * GROUP_W
CHUNK = 128
LRU_C = 8.0
N_MEM = 256
MEM_HEADS = 4
MEM_HEAD_DIM = 128
MEM_W = MEM_HEADS * MEM_HEAD_DIM
N_EXPERTS = 32
TOP_K = 4
D_EXPERT = 1024
SWIGLU_LIMIT = 7.0
SWIGLU_ALPHA = 1.702
RMS_EPS = 1e-6
PAGE = 128
LANES = 128

ROW_TILE = 512
MOE_ROWS = 512
MOE_HCHUNK = 512
VMEM_BIG = 48 * 1024 * 1024


def _cparams(sem, vmem=None):
    return pltpu.CompilerParams(dimension_semantics=sem, vmem_limit_bytes=vmem)


def _bdot(a, b):
    return jnp.dot(a.astype(BF16), b.astype(BF16), preferred_element_type=F32)


def _gelu(x):
    return 0.5 * x * (1.0 + jnp.tanh(0.7978845608028654 * (x + 0.044715 * (x * x * x))))


def _rms(x, gain):
    return x * lax.rsqrt(jnp.mean(x * x, axis=-1, keepdims=True) + RMS_EPS) * gain


def _lane_head():
    return lax.broadcasted_iota(jnp.int32, (1, GROUP_W), 1) >> HEAD_SHIFT


def _split_dot(s, ones_bf16):
    hi = s.astype(BF16)
    lo = (s - hi.astype(F32)).astype(BF16)
    return (jnp.dot(hi, ones_bf16, preferred_element_type=F32)
            + jnp.dot(lo, ones_bf16, preferred_element_type=F32))


def _norm_matmul_kernel(x_ref, g_ref, w_ref, hg_ref, o_ref, *, n_chunk, norm_cols, head_dim):
    h = _rms(x_ref[...], g_ref[...]).astype(BF16)
    n = o_ref.shape[1]
    for c0 in range(0, n, n_chunk):
        y = jnp.dot(h, w_ref[:, c0:c0 + n_chunk], preferred_element_type=F32)
        for h0 in range(0, n_chunk, head_dim):
            col = c0 + h0
            if col < norm_cols:
                yh = y[:, h0:h0 + head_dim]
                o_ref[:, col:col + head_dim] = _rms(yh, hg_ref[:, col:col + head_dim])
        if c0 + n_chunk > norm_cols:
            lo = max(c0, norm_cols)
            o_ref[:, lo:c0 + n_chunk] = y[:, lo - c0:]


def _norm_matmul(x, gain, w_bf16, head_gain=None, norm_cols=0, head_dim=LANES, n_chunk=512):
    m, k = x.shape
    n = w_bf16.shape[1]
    n_chunk = min(n_chunk, n)
    if head_gain is None:
        head_gain = jnp.ones((1, n), F32)
    kern = functools.partial(_norm_matmul_kernel, n_chunk=n_chunk, norm_cols=norm_cols,
                             head_dim=head_dim)
    return pl.pallas_call(
        kern,
        grid=(m // ROW_TILE,),
        in_specs=[pl.BlockSpec((ROW_TILE, k), lambda i: (i, 0)),
                  pl.BlockSpec((1, k), lambda i: (0, 0)),
                  pl.BlockSpec((k, n), lambda i: (0, 0)),
                  pl.BlockSpec((1, n), lambda i: (0, 0))],
        out_specs=pl.BlockSpec((ROW_TILE, n), lambda i: (i, 0)),
        out_shape=jax.ShapeDtypeStruct((m, n), F32),
        compiler_params=_cparams(("parallel",), VMEM_BIG),
        name="norm_matmul",
    )(x, gain.reshape(1, k), w_bf16, head_gain)


def _matmul_res_kernel(a_ref, w_ref, x_ref, o_ref):
    o_ref[...] = x_ref[...] + jnp.dot(a_ref[...].astype(BF16), w_ref[...],
                                      preferred_element_type=F32)


def _matmul_res(a, w_bf16, x):
    m, k = a.shape
    n = w_bf16.shape[1]
    return pl.pallas_call(
        _matmul_res_kernel,
        grid=(m // ROW_TILE,),
        in_specs=[pl.BlockSpec((ROW_TILE, k), lambda i: (i, 0)),
                  pl.BlockSpec((k, n), lambda i: (0, 0)),
                  pl.BlockSpec((ROW_TILE, n), lambda i: (i, 0))],
        out_specs=pl.BlockSpec((ROW_TILE, n), lambda i: (i, 0)),
        out_shape=jax.ShapeDtypeStruct((m, n), F32),
        compiler_params=_cparams(("parallel",), VMEM_BIG),
        name="matmul_res",
    )(a, w_bf16, x)


def _merge_kernel(ya_ref, yb_ref, yc_ref, yd_ref, g_ref, w_ref, x_ref, o_ref):
    parts = []
    for gi, y_ref in enumerate((ya_ref, yb_ref, yc_ref, yd_ref)):
        parts.append(_rms(y_ref[...], g_ref[gi:gi + 1, :]).astype(BF16))
    y = jnp.concatenate(parts, axis=1)
    o_ref[...] = x_ref[...] + jnp.dot(y, w_ref[...], preferred_element_type=F32)


def _merge(ya, yb, yc, yd, out_gain, w_bf16, x):
    m = x.shape[0]
    yspec = pl.BlockSpec((ROW_TILE, GROUP_W), lambda i: (i, 0))
    return pl.pallas_call(
        _merge_kernel,
        grid=(m // ROW_TILE,),
        in_specs=[yspec, yspec, yspec, yspec,
                  pl.BlockSpec((4, GROUP_W), lambda i: (0, 0)),
                  pl.BlockSpec((D_MODEL, D_MODEL), lambda i: (0, 0)),
                  pl.BlockSpec((ROW_TILE, D_MODEL), lambda i: (i, 0))],
        out_specs=pl.BlockSpec((ROW_TILE, D_MODEL), lambda i: (i, 0)),
        out_shape=jax.ShapeDtypeStruct((m, D_MODEL), F32),
        compiler_params=_cparams(("parallel",), VMEM_BIG),
        name="merge",
    )(ya, yb, yc, yd, out_gain, w_bf16, x)


def _shift_rows(x, s, fill):
    rolled = pltpu.roll(x, s, axis=0)
    row = lax.broadcasted_iota(jnp.int32, x.shape, 0)
    return jnp.where(row >= s, rolled, fill)


def _mixer_kernel(p_ref, h0_ref, lbuf_ref, sbuf_ref, gng_ref, ws_ref, bsf_ref, cw_ref, cb_ref,
                  wa_ref, ba_ref, wx_ref, bx_ref, lam_ref, qg_ref, kg_ref, sw_ref,
                  ya_ref, yb_ref, yd_ref, q_ref, k_ref, hl_ref, lbo_ref, sbo_ref, gv_ref,
                  xb_ref, eb_ref, h_ref, *, tb):
    t = pl.program_id(1)

    @pl.when(t == 0)
    def _():
        xb_ref[5:8, :] = lbuf_ref[0]
        eb_ref[6:8, :] = sbuf_ref[0]
        h_ref[...] = h0_ref[0]

    lane_head = _lane_head()

    a_u = p_ref[:, 0:256]
    a_v = p_ref[:, 256:512]
    v2 = _rms(_gelu(a_v), gng_ref[...])
    gv_ref[0] = v2
    mixed = bsf_ref[...]
    if tb == CHUNK:
        row = lax.broadcasted_iota(jnp.int32, (tb, tb), 0)
        col = lax.broadcasted_iota(jnp.int32, (tb, tb), 1)
        for hd in range(N_HEADS):
            w_h = jnp.where(row >= col, ws_ref[hd], 0.0)
            mixed = mixed + _bdot(w_h, jnp.where(lane_head == hd, v2, 0.0))
    else:
        row = lax.broadcasted_iota(jnp.int32, (tb, GROUP_W), 0)
        v2r = v2.astype(BF16).astype(F32)
        for s in range(tb):
            w_s = jnp.where(row >= s, ws_ref[s], 0.0).astype(BF16).astype(F32)
            mixed = mixed + w_s * v2r[s:s + 1, :]
    ya_ref[...] = _gelu(a_u) * mixed

    b_x = p_ref[:, 512:768]
    b_g = p_ref[:, 768:1024]
    xb_ref[8:8 + tb, :] = b_x
    conv = xb_ref[5:5 + tb, :] * cw_ref[0:1, :]
    for kk in range(1, 4):
        conv = conv + xb_ref[5 + kk:5 + kk + tb, :] * cw_ref[kk:kk + 1, :]
    tail = xb_ref[tb + 5:tb + 8, :]
    lbo_ref[0] = tail
    xb_ref[5:8, :] = tail
    xc = conv + cb_ref[...]
    xcb = xc.astype(BF16)
    r = jax.nn.sigmoid(jnp.dot(xcb, wa_ref[...], preferred_element_type=F32) + ba_ref[...])
    ig = jax.nn.sigmoid(jnp.dot(xcb, wx_ref[...], preferred_element_type=F32) + bx_ref[...])
    lam = lam_ref[...]
    log_sig_lam = -(jnp.maximum(-lam, 0.0) + jnp.log1p(jnp.exp(-jnp.abs(lam))))
    log_a = (LRU_C * r) * log_sig_lam
    a = jnp.exp(log_a)
    one_minus_a2 = -jnp.tanh(log_a) * (a * a + 1.0)
    bb = jnp.sqrt(one_minus_a2) * (ig * xc)
    aa = a
    s = 1
    while s < tb:
        a_sh = _shift_rows(aa, s, 1.0)
        b_sh = _shift_rows(bb, s, 0.0)
        bb = aa * b_sh + bb
        aa = aa * a_sh
        s *= 2
    hseq = aa * h_ref[...] + bb
    h_last = hseq[tb - 1:tb, :]
    h_ref[...] = h_last
    hl_ref[0] = h_last
    yb_ref[...] = hseq * _gelu(b_g)

    ones_bd = jnp.where(
        (lax.broadcasted_iota(jnp.int32, (GROUP_W, GROUP_W), 0) >> HEAD_SHIFT)
        == (lax.broadcasted_iota(jnp.int32, (GROUP_W, GROUP_W), 1) >> HEAD_SHIFT),
        1.0, 0.0).astype(BF16)
    for src, gain_ref, dst in ((1024, qg_ref, q_ref), (1280, kg_ref, k_ref)):
        xq = p_ref[:, src:src + 256]
        ms = _split_dot(xq * xq, ones_bd) * (1.0 / HEAD_DIM)
        dst[...] = xq * lax.rsqrt(ms + RMS_EPS) * gain_ref[...]

    d_b = p_ref[:, 1792:2048]
    e = p_ref[:, 2048:2304] * p_ref[:, 2304:2560]
    eb_ref[8:8 + tb, :] = e
    dconv = eb_ref[6:6 + tb, :] * sw_ref[0:1, :]
    for kk in range(1, 3):
        dconv = dconv + eb_ref[6 + kk:6 + kk + tb, :] * sw_ref[kk:kk + 1, :]
    etail = eb_ref[tb + 6:tb + 8, :]
    sbo_ref[0] = etail
    eb_ref[6:8, :] = etail
    yd_ref[...] = d_b * dconv


def _mixer_pre(p_all, row0, batch, seq, tb, h0, lbuf, sbuf, wts):
    nt = seq // tb
    rb0 = row0 // tb
    rows = batch * seq
    rmap = lambda b, t: (b * nt + t, 0)
    cmap2 = lambda b, t: (0, 0)
    cmap3 = lambda b, t: (0, 0, 0)
    smap = lambda b, t: (b, 0, 0)
    yspec = pl.BlockSpec((tb, GROUP_W), rmap)
    vec = pl.BlockSpec((1, GROUP_W), cmap2)
    wsq = pl.BlockSpec((GROUP_W, GROUP_W), cmap2)
    ws = wts["ws"]
    y_sds = jax.ShapeDtypeStruct((rows, GROUP_W), F32)
    kern = functools.partial(_mixer_kernel, tb=tb)
    return pl.pallas_call(
        kern,
        grid=(batch, nt),
        in_specs=[pl.BlockSpec((tb, PROJ_W), lambda b, t: (rb0 + b * nt + t, 0)),
                  pl.BlockSpec((1, 1, GROUP_W), smap),
                  pl.BlockSpec((1, 3, GROUP_W), smap),
                  pl.BlockSpec((1, 2, GROUP_W), smap),
                  vec,
                  pl.BlockSpec(ws.shape, cmap3),
                  pl.BlockSpec((tb, GROUP_W), cmap2),
                  pl.BlockSpec((4, GROUP_W), cmap2), vec,
                  wsq, vec, wsq, vec, vec, vec, vec,
                  pl.BlockSpec((3, GROUP_W), cmap2)],
        out_specs=[yspec, yspec, yspec, yspec, yspec,
                   pl.BlockSpec((1, 1, GROUP_W), smap),
                   pl.BlockSpec((1, 3, GROUP_W), smap),
                   pl.BlockSpec((1, 2, GROUP_W), smap),
                   pl.BlockSpec((1, tb, GROUP_W), smap)],
        out_shape=[y_sds, y_sds, y_sds, y_sds, y_sds,
                   jax.ShapeDtypeStruct((batch, 1, GROUP_W), F32),
                   jax.ShapeDtypeStruct((batch, 3, GROUP_W), F32),
                   jax.ShapeDtypeStruct((batch, 2, GROUP_W), F32),
                   jax.ShapeDtypeStruct((batch, tb, GROUP_W), F32)],
        scratch_shapes=[pltpu.VMEM((tb + 8, GROUP_W), F32),
                        pltpu.VMEM((tb + 8, GROUP_W), F32),
                        pltpu.VMEM((1, GROUP_W), F32)],
        compiler_params=_cparams(("arbitrary", "arbitrary")),
        name="mixer_pre",
    )(p_all, h0, lbuf, sbuf, wts["gng"], ws, wts["bsf"], wts["cw"], wts["cb"],
      wts["wa"], wts["ba"], wts["wx"], wts["bx"], wts["lam"], wts["qg"], wts["kg"], wts["sw"])


def _sb_setup(q, bias_ref, rows_per_head):
    nr = N_HEADS * rows_per_head
    lane_head = _lane_head()
    qs = jnp.concatenate([jnp.where(lane_head == hd, q, 0.0) for hd in range(N_HEADS)],
                         axis=0).astype(BF16)
    row_head = lax.broadcasted_iota(jnp.int32, (nr, LANES), 0) >> int(math.log2(rows_per_head))
    bias = jnp.full((nr, LANES), bias_ref[N_HEADS - 1], F32)
    for hd in range(N_HEADS - 2, -1, -1):
        bias = jnp.where(row_head == hd, bias_ref[hd], bias)
    jj = lax.broadcasted_iota(jnp.int32, (LANES, 2 * LANES), 0)
    ss = lax.broadcasted_iota(jnp.int32, (LANES, 2 * LANES), 1)
    cum = jnp.where((jj > ss) | (ss >= LANES), 1.0, 0.0).astype(BF16)
    return qs, bias, cum


def _sb_tile(qs, bias, cum, k, v, acc_ref, c_ref, mask):
    z = lax.dot_general(qs, k.astype(BF16), (((1,), (1,)), ((), ())),
                        preferred_element_type=F32) + bias
    lk = -(jnp.maximum(z, 0.0) + jnp.log1p(jnp.exp(-jnp.abs(z))))
    lkm = lk if mask is None else jnp.where(mask, lk, 0.0)
    r = _split_dot(lkm, cum)
    c = c_ref[...]
    w = jnp.exp(z + lk + r[:, :LANES] + c)
    if mask is not None:
        w = jnp.where(mask, w, 0.0)
    acc_ref[...] += jnp.dot(w.astype(BF16), v.astype(BF16), preferred_element_type=F32)
    c_ref[...] = c + r[:, LANES:]


def _sb_finish(acc_ref, o_ref, rows_per_head):
    lane_head = _lane_head()
    out = jnp.zeros((rows_per_head, GROUP_W), F32)
    for hd in range(N_HEADS):
        out = out + jnp.where(lane_head == hd,
                              acc_ref[hd * rows_per_head:(hd + 1) * rows_per_head, :], 0.0)
    o_ref[...] = out


def _sb_prompt_kernel(bias_ref, q_ref, k_ref, v_ref, o_ref, acc_ref, c_ref):
    i = pl.program_id(1)
    nr = N_HEADS * CHUNK
    qs, bias, cum = _sb_setup(q_ref[...] * (HEAD_DIM ** -0.5), bias_ref, CHUNK)
    acc_ref[...] = jnp.zeros_like(acc_ref)
    c_ref[...] = jnp.zeros_like(c_ref)
    tq = lax.broadcasted_iota(jnp.int32, (nr, LANES), 0) & (CHUNK - 1)
    ts = lax.broadcasted_iota(jnp.int32, (nr, LANES), 1)
    d0 = pl.multiple_of(i * CHUNK, CHUNK)
    _sb_tile(qs, bias, cum, k_ref[pl.ds(d0, CHUNK), :], v_ref[pl.ds(d0, CHUNK), :],
             acc_ref, c_ref, ts < tq)

    def body(j, carry):
        k0 = pl.multiple_of((i - 1 - j) * CHUNK, CHUNK)
        _sb_tile(qs, bias, cum, k_ref[pl.ds(k0, CHUNK), :], v_ref[pl.ds(k0, CHUNK), :],
                 acc_ref, c_ref, None)
        return carry

    lax.fori_loop(0, i, body, 0)
    _sb_finish(acc_ref, o_ref, CHUNK)


def _sb_prompt(sb_bias, q_p, k_p, p_all, batch, seq):
    nq = seq // CHUNK
    return pl.pallas_call(
        _sb_prompt_kernel,
        grid=(batch, nq),
        in_specs=[pl.BlockSpec(memory_space=pltpu.SMEM),
                  pl.BlockSpec((CHUNK, GROUP_W), lambda b, i: (b * nq + i, 0)),
                  pl.BlockSpec((seq, GROUP_W), lambda b, i: (b, 0)),
                  pl.BlockSpec((seq, GROUP_W), lambda b, i: (b, 6))],
        out_specs=pl.BlockSpec((CHUNK, GROUP_W), lambda b, i: (b * nq + i, 0)),
        out_shape=jax.ShapeDtypeStruct((batch * seq, GROUP_W), F32),
        scratch_shapes=[pltpu.VMEM((N_HEADS * CHUNK, GROUP_W), F32),
                        pltpu.VMEM((N_HEADS * CHUNK, LANES), F32)],
        compiler_params=_cparams(("parallel", "arbitrary")),
        name="sb_prompt",
    )(sb_bias, q_p, k_p, p_all)


def _sb_sample_kernel(pt_ref, bias_ref, q_ref, kn_ref, vn_ref, kp_ref, vp_ref, o_ref,
                      acc_ref, c_ref, kpad_ref, vpad_ref, *, tq):
    j = pl.program_id(1)
    nr = N_HEADS * tq
    qs, bias, cum = _sb_setup(q_ref[...] * (HEAD_DIM ** -0.5), bias_ref, tq)

    @pl.when(j == 0)
    def _():
        acc_ref[...] = jnp.zeros_like(acc_ref)
        c_ref[...] = jnp.zeros_like(c_ref)
        kpad_ref[...] = jnp.zeros_like(kpad_ref)
        vpad_ref[...] = jnp.zeros_like(vpad_ref)
        kpad_ref[0:tq, :] = kn_ref[...]
        vpad_ref[0:tq, :] = vn_ref[...]
        trow = lax.broadcasted_iota(jnp.int32, (nr, LANES), 0) & (tq - 1)
        ts = lax.broadcasted_iota(jnp.int32, (nr, LANES), 1)
        _sb_tile(qs, bias, cum, kpad_ref[...], vpad_ref[...], acc_ref, c_ref, ts < trow)

    @pl.when(j > 0)
    def _():
        _sb_tile(qs, bias, cum, kp_ref[0], vp_ref[0], acc_ref, c_ref, None)

    @pl.when(j == pl.num_programs(1) - 1)
    def _():
        _sb_finish(acc_ref, o_ref, tq)


def _sb_sample(page_rows, sb_bias, q_s, k_s, p_all, row0, pool_k, pool_v, batch, tq):
    n_pages = page_rows.shape[1]
    page_rows = page_rows.reshape(-1)
    rb0 = row0 // tq

    def page_map(b, j, pt):
        return (pt[b * n_pages + n_pages - jnp.maximum(j, 1)], 0, 0)

    kern = functools.partial(_sb_sample_kernel, tq=tq)
    grid_spec = pltpu.PrefetchScalarGridSpec(
        num_scalar_prefetch=1,
        grid=(batch, n_pages + 1),
        in_specs=[pl.BlockSpec(memory_space=pltpu.SMEM),
                  pl.BlockSpec((tq, GROUP_W), lambda b, j, pt: (b, 0)),
                  pl.BlockSpec((tq, GROUP_W), lambda b, j, pt: (b, 0)),
                  pl.BlockSpec((tq, GROUP_W), lambda b, j, pt: (rb0 + b, 6)),
                  pl.BlockSpec((1, PAGE, GROUP_W), page_map),
                  pl.BlockSpec((1, PAGE, GROUP_W), page_map)],
        out_specs=pl.BlockSpec((tq, GROUP_W), lambda b, j, pt: (b, 0)),
        scratch_shapes=[pltpu.VMEM((N_HEADS * tq, GROUP_W), F32),
                        pltpu.VMEM((N_HEADS * tq, LANES), F32),
                        pltpu.VMEM((PAGE, GROUP_W), F32),
                        pltpu.VMEM((PAGE, GROUP_W), F32)])
    return pl.pallas_call(
        kern,
        grid_spec=grid_spec,
        out_shape=jax.ShapeDtypeStruct((batch * tq, GROUP_W), F32),
        compiler_params=_cparams(("parallel", "arbitrary")),
        name="sb_sample",
    )(page_rows, sb_bias, q_s, k_s, p_all, pool_k, pool_v)


def _mem_attn_kernel(q_ref, k_ref, v_ref, o_ref):
    scale = MEM_HEAD_DIM ** -0.5
    for hd in range(MEM_HEADS):
        sl = slice(hd * MEM_HEAD_DIM, (hd + 1) * MEM_HEAD_DIM)
        qh = q_ref[:, sl].astype(BF16)
        kh = k_ref[0, :, sl].astype(BF16)
        s = lax.dot_general(qh, kh, (((1,), (1,)), ((), ())), preferred_element_type=F32) * scale
        s = s - jnp.max(s, axis=-1, keepdims=True)
        e = jnp.exp(s)
        p = e / jnp.sum(e, axis=-1, keepdims=True)
        o_ref[:, sl] = jnp.dot(p.astype(BF16), v_ref[0, :, sl].astype(BF16),
                               preferred_element_type=F32)


def _mem_attn(q_all, row0, batch, seq, tq, mem_k, mem_v, kv0):
    nt = seq // tq
    rb0 = row0 // tq
    kvspec = pl.BlockSpec((1, N_MEM, MEM_W), lambda b, t: (kv0 + b, 0, 0))
    return pl.pallas_call(
        _mem_attn_kernel,
        grid=(batch, nt),
        in_specs=[pl.BlockSpec((tq, MEM_W), lambda b, t: (rb0 + b * nt + t, 0)), kvspec, kvspec],
        out_specs=pl.BlockSpec((tq, MEM_W), lambda b, t: (b * nt + t, 0)),
        out_shape=jax.ShapeDtypeStruct((batch * seq, MEM_W), F32),
        compiler_params=_cparams(("parallel", "arbitrary")),
        name="mem_attn",
    )(q_all, mem_k, mem_v)


def _router_kernel(x_ref, g_ref, w_ref, b_ref, h_ref, idx_ref, gate_ref, rank_ref, cnt_ref,
                   run_ref):
    i = pl.program_id(0)

    @pl.when(i == 0)
    def _():
        run_ref[...] = jnp.zeros_like(run_ref)

    hb = _rms(x_ref[...], g_ref[...]).astype(BF16)
    h_ref[...] = hb
    logits = jnp.dot(hb, w_ref[...], preferred_element_type=F32) + b_ref[...]
    tm = logits.shape[0]
    lane = lax.broadcasted_iota(jnp.int32, logits.shape, 1).astype(F32)
    vals = jnp.where(lane < N_EXPERTS, logits, -jnp.inf)
    idx_out = jnp.zeros(logits.shape, F32)
    e_out = jnp.zeros(logits.shape, F32)
    picked = jnp.zeros(logits.shape, F32)
    sels = []
    m0 = None
    denom = None
    for kk in range(TOP_K):
        m = jnp.max(vals, axis=-1, keepdims=True)
        idx = jnp.min(jnp.where(vals == m, lane, float(LANES)), axis=-1, keepdims=True)
        sel = lane == idx
        sels.append(sel)
        picked = jnp.where(sel, 1.0, picked)
        vals = jnp.where(sel, -jnp.inf, vals)
        if kk == 0:
            m0 = m
        e = jnp.exp(m - m0)
        denom = e if kk == 0 else denom + e
        idx_out = jnp.where(lane == kk, idx, idx_out)
        e_out = jnp.where(lane == kk, e, e_out)
    idx_ref[...] = idx_out.astype(jnp.int32)
    gate_ref[...] = e_out / denom

    row = lax.broadcasted_iota(jnp.int32, (tm, tm), 0)
    col = lax.broadcasted_iota(jnp.int32, (tm, tm), 1)
    before = jnp.where(row > col, 1.0, 0.0).astype(BF16)
    prefix = jnp.dot(before, picked.astype(BF16), preferred_element_type=F32) + run_ref[...]
    rank_out = jnp.zeros(logits.shape, F32)
    for kk in range(TOP_K):
        rk = jnp.sum(jnp.where(sels[kk], prefix, 0.0), axis=-1, keepdims=True)
        rank_out = jnp.where(lane == kk, rk, rank_out)
    rank_ref[...] = rank_out.astype(jnp.int32)
    total = run_ref[...] + jnp.sum(picked, axis=0, keepdims=True)
    run_ref[...] = total
    cnt_ref[...] = total.astype(jnp.int32)


def _router(x, gain, w_bf16, bias):
    m = x.shape[0]
    rspec = pl.BlockSpec((ROW_TILE, LANES), lambda i: (i, 0))
    return pl.pallas_call(
        _router_kernel,
        grid=(m // ROW_TILE,),
        in_specs=[pl.BlockSpec((ROW_TILE, D_MODEL), lambda i: (i, 0)),
                  pl.BlockSpec((1, D_MODEL), lambda i: (0, 0)),
                  pl.BlockSpec((D_MODEL, LANES), lambda i: (0, 0)),
                  pl.BlockSpec((1, LANES), lambda i: (0, 0))],
        out_specs=[pl.BlockSpec((ROW_TILE, D_MODEL), lambda i: (i, 0)), rspec, rspec, rspec,
                   pl.BlockSpec((1, LANES), lambda i: (0, 0))],
        out_shape=[jax.ShapeDtypeStruct((m, D_MODEL), BF16),
                   jax.ShapeDtypeStruct((m, LANES), jnp.int32),
                   jax.ShapeDtypeStruct((m, LANES), F32),
                   jax.ShapeDtypeStruct((m, LANES), jnp.int32),
                   jax.ShapeDtypeStruct((1, LANES), jnp.int32)],
        scratch_shapes=[pltpu.VMEM((1, LANES), F32)],
        compiler_params=_cparams(("arbitrary",)),
        name="router",
    )(x, gain, w_bf16, bias)


def _combine_kernel(x_ref, y_ref, o_ref):
    d = x_ref.shape[1]
    o_ref[...] = x_ref[...] + ((y_ref[:, 0:d] + y_ref[:, d:2 * d])
                               + (y_ref[:, 2 * d:3 * d] + y_ref[:, 3 * d:4 * d]))


def _combine(x, y4):
    m, d = x.shape
    tm = 256
    return pl.pallas_call(
        _combine_kernel,
        grid=(m // tm,),
        in_specs=[pl.BlockSpec((tm, d), lambda i: (i, 0)),
                  pl.BlockSpec((tm, TOP_K * d), lambda i: (i, 0))],
        out_specs=pl.BlockSpec((tm, d), lambda i: (i, 0)),
        out_shape=jax.ShapeDtypeStruct((m, d), F32),
        compiler_params=_cparams(("parallel",)),
        name="combine",
    )(x, y4)


def _expert_kernel(be_ref, nb_ref, x_ref, gate_ref, wi_ref, bi_ref, wo_ref, bo_ref, o_ref):
    i = pl.program_id(0)

    @pl.when(i < nb_ref[0])
    def _():
        x = x_ref[...]
        acc = jnp.zeros(o_ref.shape, F32)
        for c0 in range(0, D_EXPERT, MOE_HCHUNK):
            g = jnp.dot(x, wi_ref[0, :, c0:c0 + MOE_HCHUNK], preferred_element_type=F32)
            g = g + bi_ref[0, :, c0:c0 + MOE_HCHUNK]
            u = jnp.dot(x, wi_ref[0, :, D_EXPERT + c0:D_EXPERT + c0 + MOE_HCHUNK],
                        preferred_element_type=F32)
            u = u + bi_ref[0, :, D_EXPERT + c0:D_EXPERT + c0 + MOE_HCHUNK]
            g = jnp.minimum(g, SWIGLU_LIMIT)
            u = jnp.clip(u, -SWIGLU_LIMIT, SWIGLU_LIMIT)
            act = g * jax.nn.sigmoid(SWIGLU_ALPHA * g) * (u + 1.0)
            acc = acc + jnp.dot(act.astype(BF16), wo_ref[0, c0:c0 + MOE_HCHUNK, :],
                                preferred_element_type=F32)
        o_ref[...] = (acc + bo_ref[0]) * gate_ref[...]

    @pl.when(i >= nb_ref[0])
    def _():
        o_ref[...] = jnp.zeros_like(o_ref)


def _experts(block_e, n_real, xb, slot_gate, wi, bi, wo, bo):
    slots = xb.shape[0]
    n_blocks = slots // MOE_ROWS
    grid_spec = pltpu.PrefetchScalarGridSpec(
        num_scalar_prefetch=2,
        grid=(n_blocks,),
        in_specs=[pl.BlockSpec((MOE_ROWS, D_MODEL), lambda i, be, nb: (i, 0)),
                  pl.BlockSpec((MOE_ROWS, 1), lambda i, be, nb: (i, 0)),
                  pl.BlockSpec((1, D_MODEL, 2 * D_EXPERT), lambda i, be, nb: (be[i], 0, 0)),
                  pl.BlockSpec((1, 1, 2 * D_EXPERT), lambda i, be, nb: (be[i], 0, 0)),
                  pl.BlockSpec((1, D_EXPERT, D_MODEL), lambda i, be, nb: (be[i], 0, 0)),
                  pl.BlockSpec((1, 1, D_MODEL), lambda i, be, nb: (be[i], 0, 0))],
        out_specs=pl.BlockSpec((MOE_ROWS, D_MODEL), lambda i, be, nb: (i, 0)))
    return pl.pallas_call(
        _expert_kernel,
        grid_spec=grid_spec,
        out_shape=jax.ShapeDtypeStruct((slots, D_MODEL), F32),
        compiler_params=_cparams(("arbitrary",), VMEM_BIG),
        name="experts",
    )(block_e, n_real, xb, slot_gate, wi, bi, wo, bo)


def _moe(x, gain, w_router, rbias, wi, bi, wo, bo):
    n = x.shape[0]
    hb, idx_pad, gate_pad, rank_pad, cnt_pad = _router(x, gain, w_router, rbias)
    n_assign = n * TOP_K
    flat_e = idx_pad[:, :TOP_K].reshape(-1)
    flat_g = gate_pad[:, :TOP_K].reshape(-1)
    flat_rank = rank_pad[:, :TOP_K].reshape(-1)
    counts = cnt_pad[0, :N_EXPERTS]
    starts = jnp.cumsum(counts) - counts
    padded = (counts + MOE_ROWS - 1) // MOE_ROWS * MOE_ROWS
    pad_ends = jnp.cumsum(padded)
    pad_starts = pad_ends - padded
    n_blocks = -(-n_assign // MOE_ROWS) + N_EXPERTS
    slots = n_blocks * MOE_ROWS
    block_start = jnp.arange(n_blocks, dtype=jnp.int32) * MOE_ROWS
    block_e = jnp.minimum(jnp.sum(pad_ends[None, :] <= block_start[:, None], axis=1),
                          N_EXPERTS - 1).astype(jnp.int32)
    n_real = (pad_ends[-1] // MOE_ROWS).astype(jnp.int32).reshape(1)
    dest = pad_starts[flat_e] + flat_rank
    _, order = lax.sort_key_val(flat_e, jnp.arange(n_assign, dtype=jnp.int32))
    slot = jnp.arange(slots, dtype=jnp.int32)
    slot_e = jnp.repeat(block_e, MOE_ROWS)
    within = slot - pad_starts[slot_e]
    valid = within < counts[slot_e]
    assign = order[jnp.clip(starts[slot_e] + within, 0, n_assign - 1)]
    slot_tok = jnp.where(valid, assign // TOP_K, 0)
    slot_gate = jnp.where(valid, flat_g[assign], 0.0)
    xb = hb[slot_tok]
    yb = _experts(block_e, n_real, xb, slot_gate.reshape(slots, 1), wi, bi, wo, bo)
    y4 = yb[dest].reshape(n, TOP_K * D_MODEL)
    return _combine(x, y4)


def _block_diag(w):
    hh, d, _ = w.shape
    eye = jnp.eye(hh, dtype=w.dtype)
    return jnp.einsum('hij,hg->higj', w, eye).reshape(hh * d, hh * d)


def kernel(x_prompt, x_sample, cache_sb_k, cache_sb_v, page_table, state_lru_h, state_lru_conv, state_sconv, cache_mem_k, cache_mem_v, mem_prompt, ln_mix, w_in, gmlp_norm, gmlp_ws, gmlp_bs, lru_conv_w, lru_conv_b, lru_wa, lru_ba, lru_wx, lru_bx, lru_lambda, sb_q_norm, sb_k_norm, sb_bias, sconv_w, out_norm, w_out, ln_mem, ln_mem_kv, mem_wq, mem_wk, mem_wv, mem_q_norm, mem_k_norm, mem_wo, ln_moe, router_w, router_b, moe_w_in, moe_b_in, moe_w_out, moe_b_out):
    depth = w_in.shape[0]
    bp, sp, _ = x_prompt.shape
    bs, ss, _ = x_sample.shape
    n_p = bp * sp
    n_s = bs * ss
    n_pool = cache_sb_k.shape[1]

    x = jnp.concatenate([x_prompt.reshape(n_p, D_MODEL), x_sample.reshape(n_s, D_MODEL)], axis=0)
    mem2d = mem_prompt.reshape(bp * N_MEM, D_MODEL)
    pool_k = cache_sb_k.reshape(depth * n_pool, PAGE, GROUP_W)
    pool_v = cache_sb_v.reshape(depth * n_pool, PAGE, GROUP_W)
    cmem_k = cache_mem_k.reshape(depth * bs, N_MEM, MEM_W)
    cmem_v = cache_mem_v.reshape(depth * bs, N_MEM, MEM_W)

    w_in_b = w_in.astype(BF16)
    w_out_b = w_out.astype(BF16)
    mem_wq_b = mem_wq.astype(BF16)
    mem_wkv_b = jnp.concatenate([mem_wk, mem_wv], axis=2).astype(BF16)
    mem_wo_b = mem_wo.astype(BF16)
    moe_wi_b = moe_w_in.astype(BF16)
    moe_wo_b = moe_w_out.astype(BF16)
    rw_b = jnp.pad(router_w, ((0, 0), (0, 0), (0, LANES - N_EXPERTS))).astype(BF16)
    rb_pad = jnp.pad(router_b, ((0, 0), (0, LANES - N_EXPERTS))).reshape(depth, 1, LANES)

    zeros_h = jnp.zeros((bp, 1, GROUP_W), F32)
    zeros_lb = jnp.zeros((bp, 3, GROUP_W), F32)
    zeros_sb = jnp.zeros((bp, 2, GROUP_W), F32)

    outs = [[] for _ in range(14)]
    for l in range(depth):
        mkv = _norm_matmul(mem2d, ln_mem_kv[l], mem_wkv_b[l],
                           head_gain=jnp.concatenate([jnp.tile(mem_k_norm[l], MEM_HEADS),
                                                      jnp.ones((MEM_W,), F32)]).reshape(1, 2 * MEM_W),
                           norm_cols=MEM_W, head_dim=MEM_HEAD_DIM)
        mk_p = mkv[:, :MEM_W].reshape(bp, N_MEM, MEM_W)
        mv_p = mkv[:, MEM_W:].reshape(bp, N_MEM, MEM_W)

        p_all = _norm_matmul(x, ln_mix[l], w_in_b[l])
        common = dict(
            gng=gmlp_norm[l].reshape(1, GROUP_W),
            cw=lru_conv_w[l], cb=lru_conv_b[l].reshape(1, GROUP_W),
            wa=_block_diag(lru_wa[l]).astype(BF16), ba=lru_ba[l].reshape(1, GROUP_W),
            wx=_block_diag(lru_wx[l]).astype(BF16), bx=lru_bx[l].reshape(1, GROUP_W),
            lam=lru_lambda[l].reshape(1, GROUP_W),
            qg=jnp.tile(sb_q_norm[l], N_HEADS).reshape(1, GROUP_W),
            kg=jnp.tile(sb_k_norm[l], N_HEADS).reshape(1, GROUP_W),
            sw=sconv_w[l])
        lp = min(CHUNK, sp)
        ls = min(CHUNK, ss)
        wts_p = dict(common, ws=gmlp_ws[l][:, :lp, :lp],
                     bsf=jnp.repeat(gmlp_bs[l][:, :lp].T, HEAD_DIM, axis=1))
        ws_exp = jnp.repeat(jnp.transpose(gmlp_ws[l][:, :ls, :ls], (2, 1, 0)), HEAD_DIM, axis=2)
        wts_s = dict(common, ws=ws_exp, bsf=jnp.repeat(gmlp_bs[l][:, :ls].T, HEAD_DIM, axis=1))
        (ya_p, yb_p, yd_p, q_p, k_p, hl_p, lb_p, sbuf_p, gv_p) = _mixer_pre(
            p_all, 0, bp, sp, lp, zeros_h, zeros_lb, zeros_sb, wts_p)
        (ya_s, yb_s, yd_s, q_s, k_s, hl_s, lb_s, sbuf_s, gv_s) = _mixer_pre(
            p_all, n_p, bs, ss, ls, state_lru_h[l].reshape(bs, 1, GROUP_W), state_lru_conv[l],
            state_sconv[l], wts_s)
        yc_p = _sb_prompt(sb_bias[l], q_p, k_p, p_all, bp, sp)
        yc_s = _sb_sample(page_table + l * n_pool, sb_bias[l], q_s, k_s, p_all, n_p,
                          pool_k, pool_v, bs, ss)
        cat = lambda a, b: jnp.concatenate([a, b], axis=0)
        x = _merge(cat(ya_p, ya_s), cat(yb_p, yb_s), cat(yc_p, yc_s), cat(yd_p, yd_s),
                   out_norm[l], w_out_b[l], x)

        q_mem = _norm_matmul(x, ln_mem[l], mem_wq_b[l],
                             head_gain=jnp.tile(mem_q_norm[l], MEM_HEADS).reshape(1, MEM_W),
                             norm_cols=MEM_W, head_dim=MEM_HEAD_DIM)
        a_p = _mem_attn(q_mem, 0, bp, sp, 512, mk_p, mv_p, 0)
        a_s = _mem_attn(q_mem, n_p, bs, ss, ss, cmem_k, cmem_v, l * bs)
        x = _matmul_res(cat(a_p, a_s), mem_wo_b[l], x)

        x = _moe(x, ln_moe[l].reshape(1, D_MODEL), rw_b[l], rb_pad[l],
                 moe_wi_b[l], moe_b_in[l].reshape(N_EXPERTS, 1, 2 * D_EXPERT),
                 moe_wo_b[l], moe_b_out[l].reshape(N_EXPERTS, 1, D_MODEL))

        v_all = p_all[:, 1536:1792]
        outs[0].append(k_p.reshape(bp, sp, N_HEADS, HEAD_DIM))
        outs[1].append(v_all[:n_p].reshape(bp, sp, N_HEADS, HEAD_DIM))
        outs[2].append(k_s.reshape(bs, ss, N_HEADS, HEAD_DIM))
        outs[3].append(v_all[n_p:].reshape(bs, ss, N_HEADS, HEAD_DIM))
        outs[4].append(hl_p.reshape(bp, GROUP_W))
        outs[5].append(hl_s.reshape(bs, GROUP_W))
        outs[6].append(lb_p)
        outs[7].append(lb_s)
        outs[8].append(sbuf_p)
        outs[9].append(sbuf_s)
        outs[10].append(gv_p.reshape(bp, lp, N_HEADS, HEAD_DIM))
        outs[11].append(gv_s.reshape(bs, ls, N_HEADS, HEAD_DIM))
        outs[12].append(mk_p.reshape(bp, N_MEM, MEM_HEADS, MEM_HEAD_DIM))
        outs[13].append(mv_p.reshape(bp, N_MEM, MEM_HEADS, MEM_HEAD_DIM))

    y_prompt = x[:n_p].reshape(bp, sp, D_MODEL)
    y_sample = x[n_p:].reshape(bs, ss, D_MODEL)
    return (y_prompt, y_sample) + tuple(jnp.stack(o) for o in outs)
```

```python
import functools
import math

import jax
import jax.numpy as jnp
from jax import lax
from jax.experimental import pallas as pl
from jax.experimental.pallas import tpu as pltpu

F32 = jnp.float32
BF16 = jnp.bfloat16

D_MODEL = 1024
GROUP_W = 256
N_HEADS = 4
HEAD_DIM = 64
HEAD_SHIFT = 6
PROJ_W = 10 * GROUP_W
CHUNK = 128
LRU_C = 8.0
N_MEM = 256
MEM_HEADS = 4
MEM_HEAD_DIM = 128
MEM_W = MEM_HEADS * MEM_HEAD_DIM
N_EXPERTS = 32
TOP_K = 4
D_EXPERT = 1024
SWIGLU_LIMIT = 7.0
SWIGLU_ALPHA = 1.702
RMS_EPS = 1e-6
PAGE = 128
LANES = 128

ROW_TILE = 512
MOE_ROWS = 512
MOE_HCHUNK = 512
VMEM_BIG = 48 * 1024 * 1024


def _cparams(sem, vmem=None):
    return pltpu.CompilerParams(dimension_semantics=sem, vmem_limit_bytes=vmem)


def _bdot(a, b):
    return jnp.dot(a.astype(BF16), b.astype(BF16), preferred_element_type=F32)


def _gelu(x):
    return 0.5 * x * (1.0 + jnp.tanh(0.7978845608028654 * (x + 0.044715 * (x * x * x))))


def _rms(x, gain):
    return x * lax.rsqrt(jnp.mean(x * x, axis=-1, keepdims=True) + RMS_EPS) * gain


def _lane_head():
    return lax.broadcasted_iota(jnp.int32, (1, GROUP_W), 1) >> HEAD_SHIFT


def _split_dot(s, ones_bf16):
    hi = s.astype(BF16)
    lo = (s - hi.astype(F32)).astype(BF16)
    return (jnp.dot(hi, ones_bf16, preferred_element_type=F32)
            + jnp.dot(lo, ones_bf16, preferred_element_type=F32))


def _norm_matmul_kernel(x_ref, g_ref, w_ref, hg_ref, o_ref, *, n_chunk, norm_cols, head_dim):
    h = _rms(x_ref[...], g_ref[...]).astype(BF16)
    n = o_ref.shape[1]
    for c0 in range(0, n, n_chunk):
        y = jnp.dot(h, w_ref[:, c0:c0 + n_chunk], preferred_element_type=F32)
        for h0 in range(0, n_chunk, head_dim):
            col = c0 + h0
            if col < norm_cols:
                yh = y[:, h0:h0 + head_dim]
                o_ref[:, col:col + head_dim] = _rms(yh, hg_ref[:, col:col + head_dim])
        if c0 + n_chunk > norm_cols:
            lo = max(c0, norm_cols)
            o_ref[:, lo:c0 + n_chunk] = y[:, lo - c0:]


def _norm_matmul(x, gain, w_bf16, head_gain=None, norm_cols=0, head_dim=LANES, n_chunk=512):
    m, k = x.shape
    n = w_bf16.shape[1]
    n_chunk = min(n_chunk, n)
    if head_gain is None:
        head_gain = jnp.ones((1, n), F32)
    kern = functools.partial(_norm_matmul_kernel, n_chunk=n_chunk, norm_cols=norm_cols,
                             head_dim=head_dim)
    return pl.pallas_call(
        kern,
        grid=(m // ROW_TILE,),
        in_specs=[pl.BlockSpec((ROW_TILE, k), lambda i: (i, 0)),
                  pl.BlockSpec((1, k), lambda i: (0, 0)),
                  pl.BlockSpec((k, n), lambda i: (0, 0)),
                  pl.BlockSpec((1, n), lambda i: (0, 0))],
        out_specs=pl.BlockSpec((ROW_TILE, n), lambda i: (i, 0)),
        out_shape=jax.ShapeDtypeStruct((m, n), F32),
        compiler_params=_cparams(("parallel",), VMEM_BIG),
        name="norm_matmul",
    )(x, gain.reshape(1, k), w_bf16, head_gain)


def _matmul_res_kernel(a_ref, w_ref, x_ref, o_ref):
    o_ref[...] = x_ref[...] + jnp.dot(a_ref[...].astype(BF16), w_ref[...],
                                      preferred_element_type=F32)


def _matmul_res(a, w_bf16, x):
    m, k = a.shape
    n = w_bf16.shape[1]
    return pl.pallas_call(
        _matmul_res_kernel,
        grid=(m // ROW_TILE,),
        in_specs=[pl.BlockSpec((ROW_TILE, k), lambda i: (i, 0)),
                  pl.BlockSpec((k, n), lambda i: (0, 0)),
                  pl.BlockSpec((ROW_TILE, n), lambda i: (i, 0))],
        out_specs=pl.BlockSpec((ROW_TILE, n), lambda i: (i, 0)),
        out_shape=jax.ShapeDtypeStruct((m, n), F32),
        compiler_params=_cparams(("parallel",), VMEM_BIG),
        name="matmul_res",
    )(a, w_bf16, x)


def _merge_kernel(ya_ref, yb_ref, yc_ref, yd_ref, g_ref, w_ref, x_ref, o_ref):
    parts = []
    for gi, y_ref in enumerate((ya_ref, yb_ref, yc_ref, yd_ref)):
        parts.append(_rms(y_ref[...], g_ref[gi:gi + 1, :]).astype(BF16))
    y = jnp.concatenate(parts, axis=1)
    o_ref[...] = x_ref[...] + jnp.dot(y, w_ref[...], preferred_element_type=F32)


def _merge(ya, yb, yc, yd, out_gain, w_bf16, x):
    m = x.shape[0]
    yspec = pl.BlockSpec((ROW_TILE, GROUP_W), lambda i: (i, 0))
    return pl.pallas_call(
        _merge_kernel,
        grid=(m // ROW_TILE,),
        in_specs=[yspec, yspec, yspec, yspec,
                  pl.BlockSpec((4, GROUP_W), lambda i: (0, 0)),
                  pl.BlockSpec((D_MODEL, D_MODEL), lambda i: (0, 0)),
                  pl.BlockSpec((ROW_TILE, D_MODEL), lambda i: (i, 0))],
        out_specs=pl.BlockSpec((ROW_TILE, D_MODEL), lambda i: (i, 0)),
        out_shape=jax.ShapeDtypeStruct((m, D_MODEL), F32),
        compiler_params=_cparams(("parallel",), VMEM_BIG),
        name="merge",
    )(ya, yb, yc, yd, out_gain, w_bf16, x)


def _shift_rows(x, s, fill):
    rolled = pltpu.roll(x, s, axis=0)
    row = lax.broadcasted_iota(jnp.int32, x.shape, 0)
    return jnp.where(row >= s, rolled, fill)


def _mixer_kernel(p_ref, h0_ref, lbuf_ref, sbuf_ref, gng_ref, ws_ref, bsf_ref, cw_ref, cb_ref,
                  wa_ref, ba_ref, wx_ref, bx_ref, lam_ref, qg_ref, kg_ref, sw_ref,
                  ya_ref, yb_ref, yd_ref, q_ref, k_ref, hl_ref, lbo_ref, sbo_ref, gv_ref,
                  xb_ref, eb_ref, h_ref, *, tb):
    t = pl.program_id(1)

    @pl.when(t == 0)
    def _():
        xb_ref[5:8, :] = lbuf_ref[0]
        eb_ref[6:8, :] = sbuf_ref[0]
        h_ref[...] = h0_ref[0]

    lane_head = _lane_head()

    a_u = p_ref[:, 0:256]
    a_v = p_ref[:, 256:512]
    v2 = _rms(_gelu(a_v), gng_ref[...])
    gv_ref[0] = v2
    mixed = bsf_ref[...]
    if tb == CHUNK:
        row = lax.broadcasted_iota(jnp.int32, (tb, tb), 0)
        col = lax.broadcasted_iota(jnp.int32, (tb, tb), 1)
        for hd in range(N_HEADS):
            w_h = jnp.where(row >= col, ws_ref[hd], 0.0)
            mixed = mixed + _bdot(w_h, jnp.where(lane_head == hd, v2, 0.0))
    else:
        row = lax.broadcasted_iota(jnp.int32, (tb, GROUP_W), 0)
        v2r = v2.astype(BF16).astype(F32)
        for s in range(tb):
            w_s = jnp.where(row >= s, ws_ref[s], 0.0).astype(BF16).astype(F32)
            mixed = mixed + w_s * v2r[s:s + 1, :]
    ya_ref[...] = _gelu(a_u) * mixed

    b_x = p_ref[:, 512:768]
    b_g = p_ref[:, 768:1024]
    xb_ref[8:8 + tb, :] = b_x
    conv = xb_ref[5:5 + tb, :] * cw_ref[0:1, :]
    for kk in range(1, 4):
        conv = conv + xb_ref[5 + kk:5 + kk + tb, :] * cw_ref[kk:kk + 1, :]
    tail = xb_ref[tb + 5:tb + 8, :]
    lbo_ref[0] = tail
    xb_ref[5:8, :] = tail
    xc = conv + cb_ref[...]
    xcb = xc.astype(BF16)
    r = jax.nn.sigmoid(jnp.dot(xcb, wa_ref[...], preferred_element_type=F32) + ba_ref[...])
    ig = jax.nn.sigmoid(jnp.dot(xcb, wx_ref[...], preferred_element_type=F32) + bx_ref[...])
    lam = lam_ref[...]
    log_sig_lam = -(jnp.maximum(-lam, 0.0) + jnp.log1p(jnp.exp(-jnp.abs(lam))))
    log_a = (LRU_C * r) * log_sig_lam
    a = jnp.exp(log_a)
    one_minus_a2 = -jnp.tanh(log_a) * (a * a + 1.0)
    bb = jnp.sqrt(one_minus_a2) * (ig * xc)
    aa = a
    s = 1
    while s < tb:
        a_sh = _shift_rows(aa, s, 1.0)
        b_sh = _shift_rows(bb, s, 0.0)
        bb = aa * b_sh + bb
        aa = aa * a_sh
        s *= 2
    hseq = aa * h_ref[...] + bb
    h_last = hseq[tb - 1:tb, :]
    h_ref[...] = h_last
    hl_ref[0] = h_last
    yb_ref[...] = hseq * _gelu(b_g)

    ones_bd = jnp.where(
        (lax.broadcasted_iota(jnp.int32, (GROUP_W, GROUP_W), 0) >> HEAD_SHIFT)
        == (lax.broadcasted_iota(jnp.int32, (GROUP_W, GROUP_W), 1) >> HEAD_SHIFT),
        1.0, 0.0).astype(BF16)
    for src, gain_ref, dst in ((1024, qg_ref, q_ref), (1280, kg_ref, k_ref)):
        xq = p_ref[:, src:src + 256]
        ms = _split_dot(xq * xq, ones_bd) * (1.0 / HEAD_DIM)
        dst[...] = xq * lax.rsqrt(ms + RMS_EPS) * gain_ref[...]

    d_b = p_ref[:, 1792:2048]
    e = p_ref[:, 2048:2304] * p_ref[:, 2304:2560]
    eb_ref[8:8 + tb, :] = e
    dconv = eb_ref[6:6 + tb, :] * sw_ref[0:1, :]
    for kk in range(1, 3):
        dconv = dconv + eb_ref[6 + kk:6 + kk + tb, :] * sw_ref[kk:kk + 1, :]
    etail = eb_ref[tb + 6:tb + 8, :]
    sbo_ref[0] = etail
    eb_ref[6:8, :] = etail
    yd_ref[...] = d_b * dconv


def _mixer_pre(p_all, row0, batch, seq, tb, h0, lbuf, sbuf, wts):
    nt = seq // tb
    rb0 = row0 // tb
    rows = batch * seq
    rmap = lambda b, t: (b * nt + t, 0)
    cmap2 = lambda b, t: (0, 0)
    cmap3 = lambda b, t: (0, 0, 0)
    smap = lambda b, t: (b, 0, 0)
    yspec = pl.BlockSpec((tb, GROUP_W), rmap)
    vec = pl.BlockSpec((1, GROUP_W), cmap2)
    wsq = pl.BlockSpec((GROUP_W, GROUP_W), cmap2)
    ws = wts["ws"]
    y_sds = jax.ShapeDtypeStruct((rows, GROUP_W), F32)
    kern = functools.partial(_mixer_kernel, tb=tb)
    return pl.pallas_call(
        kern,
        grid=(batch, nt),
        in_specs=[pl.BlockSpec((tb, PROJ_W), lambda b, t: (rb0 + b * nt + t, 0)),
                  pl.BlockSpec((1, 1, GROUP_W), smap),
                  pl.BlockSpec((1, 3, GROUP_W), smap),
                  pl.BlockSpec((1, 2, GROUP_W), smap),
                  vec,
                  pl.BlockSpec(ws.shape, cmap3),
                  pl.BlockSpec((tb, GROUP_W), cmap2),
                  pl.BlockSpec((4, GROUP_W), cmap2), vec,
                  wsq, vec, wsq, vec, vec, vec, vec,
                  pl.BlockSpec((3, GROUP_W), cmap2)],
        out_specs=[yspec, yspec, yspec, yspec, yspec,
                   pl.BlockSpec((1, 1, GROUP_W), smap),
                   pl.BlockSpec((1, 3, GROUP_W), smap),
                   pl.BlockSpec((1, 2, GROUP_W), smap),
                   pl.BlockSpec((1, tb, GROUP_W), smap)],
        out_shape=[y_sds, y_sds, y_sds, y_sds, y_sds,
                   jax.ShapeDtypeStruct((batch, 1, GROUP_W), F32),
                   jax.ShapeDtypeStruct((batch, 3, GROUP_W), F32),
                   jax.ShapeDtypeStruct((batch, 2, GROUP_W), F32),
                   jax.ShapeDtypeStruct((batch, tb, GROUP_W), F32)],
        scratch_shapes=[pltpu.VMEM((tb + 8, GROUP_W), F32),
                        pltpu.VMEM((tb + 8, GROUP_W), F32),
                        pltpu.VMEM((1, GROUP_W), F32)],
        compiler_params=_cparams(("arbitrary", "arbitrary")),
        name="mixer_pre",
    )(p_all, h0, lbuf, sbuf, wts["gng"], ws, wts["bsf"], wts["cw"], wts["cb"],
      wts["wa"], wts["ba"], wts["wx"], wts["bx"], wts["lam"], wts["qg"], wts["kg"], wts["sw"])


def _sb_setup(q, bias_ref, rows_per_head):
    nr = N_HEADS * rows_per_head
    lane_head = _lane_head()
    qs = jnp.concatenate([jnp.where(lane_head == hd, q, 0.0) for hd in range(N_HEADS)],
                         axis=0).astype(BF16)
    row_head = lax.broadcasted_iota(jnp.int32, (nr, LANES), 0) >> int(math.log2(rows_per_head))
    bias = jnp.full((nr, LANES), bias_ref[N_HEADS - 1], F32)
    for hd in range(N_HEADS - 2, -1, -1):
        bias = jnp.where(row_head == hd, bias_ref[hd], bias)
    jj = lax.broadcasted_iota(jnp.int32, (LANES, 2 * LANES), 0)
    ss = lax.broadcasted_iota(jnp.int32, (LANES, 2 * LANES), 1)
    cum = jnp.where((jj > ss) | (ss >= LANES), 1.0, 0.0).astype(BF16)
    return qs, bias, cum


def _sb_tile(qs, bias, cum, k, v, acc, c, mask):
    z = lax.dot_general(qs, k.astype(BF16), (((1,), (1,)), ((), ())),
                        preferred_element_type=F32) + bias
    lk = -(jnp.maximum(z, 0.0) + jnp.log1p(jnp.exp(-jnp.abs(z))))
    lkm = lk if mask is None else jnp.where(mask, lk, 0.0)
    r = _split_dot(lkm, cum)
    w = jnp.exp(z + lk + r[:, :LANES] + c)
    if mask is not None:
        w = jnp.where(mask, w, 0.0)
    acc = acc + jnp.dot(w.astype(BF16), v.astype(BF16), preferred_element_type=F32)
    return acc, c + r[:, LANES:]


def _sb_tile_ref(qs, bias, cum, k, v, acc_ref, c_ref, mask):
    acc, c = _sb_tile(qs, bias, cum, k, v, acc_ref[...], c_ref[...], mask)
    acc_ref[...] = acc
    c_ref[...] = c


def _sb_finish(acc, o_ref, rows_per_head):
    lane_head = _lane_head()
    out = jnp.zeros((rows_per_head, GROUP_W), F32)
    for hd in range(N_HEADS):
        out = out + jnp.where(lane_head == hd,
                              acc[hd * rows_per_head:(hd + 1) * rows_per_head, :], 0.0)
    o_ref[...] = out


def _sb_prompt_kernel(bias_ref, q_ref, k_ref, v_ref, o_ref, acc_ref, c_ref):
    i = pl.program_id(1)
    nr = N_HEADS * CHUNK
    qs, bias, cum = _sb_setup(q_ref[...] * (HEAD_DIM ** -0.5), bias_ref, CHUNK)
    acc_ref[...] = jnp.zeros_like(acc_ref)
    c_ref[...] = jnp.zeros_like(c_ref)
    tq = lax.broadcasted_iota(jnp.int32, (nr, LANES), 0) & (CHUNK - 1)
    ts = lax.broadcasted_iota(jnp.int32, (nr, LANES), 1)
    d0 = pl.multiple_of(i * CHUNK, CHUNK)
    _sb_tile_ref(qs, bias, cum, k_ref[pl.ds(d0, CHUNK), :], v_ref[pl.ds(d0, CHUNK), :],
                 acc_ref, c_ref, ts < tq)

    def body(j, carry):
        k0 = pl.multiple_of((i - 1 - j) * CHUNK, CHUNK)
        _sb_tile_ref(qs, bias, cum, k_ref[pl.ds(k0, CHUNK), :], v_ref[pl.ds(k0, CHUNK), :],
                     acc_ref, c_ref, None)
        return carry

    lax.fori_loop(0, i, body, 0)
    _sb_finish(acc_ref[...], o_ref, CHUNK)


def _sb_prompt(sb_bias, q_p, k_p, p_all, batch, seq):
    nq = seq // CHUNK
    return pl.pallas_call(
        _sb_prompt_kernel,
        grid=(batch, nq),
        in_specs=[pl.BlockSpec(memory_space=pltpu.SMEM),
                  pl.BlockSpec((CHUNK, GROUP_W), lambda b, i: (b * nq + i, 0)),
                  pl.BlockSpec((seq, GROUP_W), lambda b, i: (b, 0)),
                  pl.BlockSpec((seq, GROUP_W), lambda b, i: (b, 6))],
        out_specs=pl.BlockSpec((CHUNK, GROUP_W), lambda b, i: (b * nq + i, 0)),
        out_shape=jax.ShapeDtypeStruct((batch * seq, GROUP_W), F32),
        scratch_shapes=[pltpu.VMEM((N_HEADS * CHUNK, GROUP_W), F32),
                        pltpu.VMEM((N_HEADS * CHUNK, LANES), F32)],
        compiler_params=_cparams(("parallel", "arbitrary")),
        name="sb_prompt",
    )(sb_bias, q_p, k_p, p_all)


def _sb_sample_kernel(pt_ref, bias_ref, q_ref, kn_ref, vn_ref, *rest, tq, n_pages):
    kp_refs = rest[:n_pages]
    vp_refs = rest[n_pages:2 * n_pages]
    o_ref, kpad_ref, vpad_ref = rest[2 * n_pages:]
    nr = N_HEADS * tq
    qs, bias, cum = _sb_setup(q_ref[...] * (HEAD_DIM ** -0.5), bias_ref, tq)
    kpad_ref[...] = jnp.zeros_like(kpad_ref)
    vpad_ref[...] = jnp.zeros_like(vpad_ref)
    kpad_ref[0:tq, :] = kn_ref[...]
    vpad_ref[0:tq, :] = vn_ref[...]
    trow = lax.broadcasted_iota(jnp.int32, (nr, LANES), 0) & (tq - 1)
    ts = lax.broadcasted_iota(jnp.int32, (nr, LANES), 1)
    acc = jnp.zeros((nr, GROUP_W), F32)
    c = jnp.zeros((nr, LANES), F32)
    acc, c = _sb_tile(qs, bias, cum, kpad_ref[...], vpad_ref[...], acc, c, ts < trow)
    for j in range(n_pages - 1, -1, -1):
        acc, c = _sb_tile(qs, bias, cum, kp_refs[j][0], vp_refs[j][0], acc, c, None)
    _sb_finish(acc, o_ref, tq)


def _sb_sample(page_rows, sb_bias, q_s, k_s, p_all, row0, pool_k, pool_v, batch, tq):
    n_pages = page_rows.shape[1]
    page_rows = page_rows.reshape(-1)
    rb0 = row0 // tq

    def page_spec(j):
        return pl.BlockSpec((1, PAGE, GROUP_W), lambda b, pt: (pt[b * n_pages + j], 0, 0))

    kern = functools.partial(_sb_sample_kernel, tq=tq, n_pages=n_pages)
    grid_spec = pltpu.PrefetchScalarGridSpec(
        num_scalar_prefetch=1,
        grid=(batch,),
        in_specs=([pl.BlockSpec(memory_space=pltpu.SMEM),
                   pl.BlockSpec((tq, GROUP_W), lambda b, pt: (b, 0)),
                   pl.BlockSpec((tq, GROUP_W), lambda b, pt: (b, 0)),
                   pl.BlockSpec((tq, GROUP_W), lambda b, pt: (rb0 + b, 6))]
                  + [page_spec(j) for j in range(n_pages)]
                  + [page_spec(j) for j in range(n_pages)]),
        out_specs=pl.BlockSpec((tq, GROUP_W), lambda b, pt: (b, 0)),
        scratch_shapes=[pltpu.VMEM((PAGE, GROUP_W), F32),
                        pltpu.VMEM((PAGE, GROUP_W), F32)])
    return pl.pallas_call(
        kern,
        grid_spec=grid_spec,
        out_shape=jax.ShapeDtypeStruct((batch * tq, GROUP_W), F32),
        compiler_params=_cparams(("parallel",)),
        name="sb_sample",
    )(page_rows, sb_bias, q_s, k_s, p_all, *([pool_k] * n_pages), *([pool_v] * n_pages))


def _mem_attn_kernel(q_ref, k_ref, v_ref, o_ref):
    scale = MEM_HEAD_DIM ** -0.5
    for hd in range(MEM_HEADS):
        sl = slice(hd * MEM_HEAD_DIM, (hd + 1) * MEM_HEAD_DIM)
        qh = q_ref[:, sl].astype(BF16)
        kh = k_ref[0, :, sl].astype(BF16)
        s = lax.dot_general(qh, kh, (((1,), (1,)), ((), ())), preferred_element_type=F32) * scale
        s = s - jnp.max(s, axis=-1, keepdims=True)
        e = jnp.exp(s)
        p = e / jnp.sum(e, axis=-1, keepdims=True)
        o_ref[:, sl] = jnp.dot(p.astype(BF16), v_ref[0, :, sl].astype(BF16),
                               preferred_element_type=F32)


def _mem_attn(q_all, row0, batch, seq, tq, mem_k, mem_v, kv0):
    nt = seq // tq
    rb0 = row0 // tq
    kvspec = pl.BlockSpec((1, N_MEM, MEM_W), lambda b, t: (kv0 + b, 0, 0))
    return pl.pallas_call(
        _mem_attn_kernel,
        grid=(batch, nt),
        in_specs=[pl.BlockSpec((tq, MEM_W), lambda b, t: (rb0 + b * nt + t, 0)), kvspec, kvspec],
        out_specs=pl.BlockSpec((tq, MEM_W), lambda b, t: (b * nt + t, 0)),
        out_shape=jax.ShapeDtypeStruct((batch * seq, MEM_W), F32),
        compiler_params=_cparams(("parallel", "arbitrary")),
        name="mem_attn",
    )(q_all, mem_k, mem_v)


def _router_kernel(x_ref, g_ref, w_ref, b_ref, h_ref, idx_ref, gate_ref, rank_ref, cnt_ref,
                   run_ref):
    i = pl.program_id(0)

    @pl.when(i == 0)
    def _():
        run_ref[...] = jnp.zeros_like(run_ref)

    hb = _rms(x_ref[...], g_ref[...]).astype(BF16)
    h_ref[...] = hb.astype(F32)
    logits = jnp.dot(hb, w_ref[...], preferred_element_type=F32) + b_ref[...]
    tm = logits.shape[0]
    lane = lax.broadcasted_iota(jnp.int32, logits.shape, 1).astype(F32)
    vals = jnp.where(lane < N_EXPERTS, logits, -jnp.inf)
    idx_out = jnp.zeros(logits.shape, F32)
    e_out = jnp.zeros(logits.shape, F32)
    picked = jnp.zeros(logits.shape, F32)
    sels = []
    m0 = None
    denom = None
    for kk in range(TOP_K):
        m = jnp.max(vals, axis=-1, keepdims=True)
        idx = jnp.min(jnp.where(vals == m, lane, float(LANES)), axis=-1, keepdims=True)
        sel = lane == idx
        sels.append(sel)
        picked = jnp.where(sel, 1.0, picked)
        vals = jnp.where(sel, -jnp.inf, vals)
        if kk == 0:
            m0 = m
        e = jnp.exp(m - m0)
        denom = e if kk == 0 else denom + e
        idx_out = jnp.where(lane == kk, idx, idx_out)
        e_out = jnp.where(lane == kk, e, e_out)
    idx_ref[...] = idx_out.astype(jnp.int32)
    gate_ref[...] = e_out / denom

    row = lax.broadcasted_iota(jnp.int32, (tm, tm), 0)
    col = lax.broadcasted_iota(jnp.int32, (tm, tm), 1)
    before = jnp.where(row > col, 1.0, 0.0).astype(BF16)
    prefix = jnp.dot(before, picked.astype(BF16), preferred_element_type=F32) + run_ref[...]
    rank_out = jnp.zeros(logits.shape, F32)
    for kk in range(TOP_K):
        rk = jnp.sum(jnp.where(sels[kk], prefix, 0.0), axis=-1, keepdims=True)
        rank_out = jnp.where(lane == kk, rk, rank_out)
    rank_ref[...] = rank_out.astype(jnp.int32)
    total = run_ref[...] + jnp.sum(picked, axis=0, keepdims=True)
    run_ref[...] = total
    cnt_ref[...] = total.astype(jnp.int32)


def _router(x, gain, w_bf16, bias):
    m = x.shape[0]
    rspec = pl.BlockSpec((ROW_TILE, LANES), lambda i: (i, 0))
    return pl.pallas_call(
        _router_kernel,
        grid=(m // ROW_TILE,),
        in_specs=[pl.BlockSpec((ROW_TILE, D_MODEL), lambda i: (i, 0)),
                  pl.BlockSpec((1, D_MODEL), lambda i: (0, 0)),
                  pl.BlockSpec((D_MODEL, LANES), lambda i: (0, 0)),
                  pl.BlockSpec((1, LANES), lambda i: (0, 0))],
        out_specs=[pl.BlockSpec((ROW_TILE, D_MODEL), lambda i: (i, 0)), rspec, rspec, rspec,
                   pl.BlockSpec((1, LANES), lambda i: (0, 0))],
        out_shape=[jax.ShapeDtypeStruct((m, D_MODEL), F32),
                   jax.ShapeDtypeStruct((m, LANES), jnp.int32),
                   jax.ShapeDtypeStruct((m, LANES), F32),
                   jax.ShapeDtypeStruct((m, LANES), jnp.int32),
                   jax.ShapeDtypeStruct((1, LANES), jnp.int32)],
        scratch_shapes=[pltpu.VMEM((1, LANES), F32)],
        compiler_params=_cparams(("arbitrary",)),
        name="router",
    )(x, gain, w_bf16, bias)


def _combine_kernel(x_ref, y_ref, o_ref):
    d = x_ref.shape[1]
    o_ref[...] = x_ref[...] + ((y_ref[:, 0:d] + y_ref[:, d:2 * d])
                               + (y_ref[:, 2 * d:3 * d] + y_ref[:, 3 * d:4 * d]))


def _combine(x, y4):
    m, d = x.shape
    tm = 256
    return pl.pallas_call(
        _combine_kernel,
        grid=(m // tm,),
        in_specs=[pl.BlockSpec((tm, d), lambda i: (i, 0)),
                  pl.BlockSpec((tm, TOP_K * d), lambda i: (i, 0))],
        out_specs=pl.BlockSpec((tm, d), lambda i: (i, 0)),
        out_shape=jax.ShapeDtypeStruct((m, d), F32),
        compiler_params=_cparams(("parallel",)),
        name="combine",
    )(x, y4)


def _expert_kernel(be_ref, nb_ref, tok_ref, tok_next_ref, h_hbm, gate_ref, wi_ref, bi_ref,
                   wo_ref, bo_ref, o_ref, xbuf_ref, sem_ref):
    i = pl.program_id(0)
    nb = nb_ref[0]
    cur = i % 2

    def issue(idx_ref, buf):
        def body(r, carry):
            tok = idx_ref[0, 0, r]
            pltpu.make_async_copy(h_hbm.at[pl.ds(tok, 1), :],
                                  xbuf_ref.at[buf, pl.ds(r, 1), :], sem_ref.at[buf]).start()
            return carry
        lax.fori_loop(0, MOE_ROWS, body, 0, unroll=8)

    @pl.when(jnp.logical_and(i == 0, nb > 0))
    def _():
        issue(tok_ref, 0)

    @pl.when(i + 1 < nb)
    def _():
        issue(tok_next_ref, 1 - cur)

    @pl.when(i < nb)
    def _():
        pltpu.make_async_copy(h_hbm.at[pl.ds(0, MOE_ROWS), :], xbuf_ref.at[cur],
                              sem_ref.at[cur]).wait()
        x = xbuf_ref[cur].astype(BF16)
        acc = jnp.zeros(o_ref.shape, F32)
        for c0 in range(0, D_EXPERT, MOE_HCHUNK):
            g = jnp.dot(x, wi_ref[0, :, c0:c0 + MOE_HCHUNK], preferred_element_type=F32)
            g = g + bi_ref[0, :, c0:c0 + MOE_HCHUNK]
            u = jnp.dot(x, wi_ref[0, :, D_EXPERT + c0:D_EXPERT + c0 + MOE_HCHUNK],
                        preferred_element_type=F32)
            u = u + bi_ref[0, :, D_EXPERT + c0:D_EXPERT + c0 + MOE_HCHUNK]
            g = jnp.minimum(g, SWIGLU_LIMIT)
            u = jnp.clip(u, -SWIGLU_LIMIT, SWIGLU_LIMIT)
            act = g * jax.nn.sigmoid(SWIGLU_ALPHA * g) * (u + 1.0)
            acc = acc + jnp.dot(act.astype(BF16), wo_ref[0, c0:c0 + MOE_HCHUNK, :],
                                preferred_element_type=F32)
        o_ref[...] = (acc + bo_ref[0]) * gate_ref[...]

    @pl.when(i >= nb_ref[0])
    def _():
        o_ref[...] = jnp.zeros_like(o_ref)


def _experts(block_e, n_real, slot_tok, h, slot_gate, wi, bi, wo, bo):
    slots = slot_tok.shape[0]
    n_blocks = slots // MOE_ROWS
    tok3 = slot_tok.reshape(n_blocks, 1, MOE_ROWS)
    grid_spec = pltpu.PrefetchScalarGridSpec(
        num_scalar_prefetch=2,
        grid=(n_blocks,),
        in_specs=[pl.BlockSpec((1, 1, MOE_ROWS), lambda i, be, nb: (i, 0, 0),
                               memory_space=pltpu.SMEM),
                  pl.BlockSpec((1, 1, MOE_ROWS),
                               lambda i, be, nb: (jnp.minimum(i + 1, n_blocks - 1), 0, 0),
                               memory_space=pltpu.SMEM),
                  pl.BlockSpec(memory_space=pl.ANY),
                  pl.BlockSpec((MOE_ROWS, 1), lambda i, be, nb: (i, 0)),
                  pl.BlockSpec((1, D_MODEL, 2 * D_EXPERT), lambda i, be, nb: (be[i], 0, 0)),
                  pl.BlockSpec((1, 1, 2 * D_EXPERT), lambda i, be, nb: (be[i], 0, 0)),
                  pl.BlockSpec((1, D_EXPERT, D_MODEL), lambda i, be, nb: (be[i], 0, 0)),
                  pl.BlockSpec((1, 1, D_MODEL), lambda i, be, nb: (be[i], 0, 0))],
        out_specs=pl.BlockSpec((MOE_ROWS, D_MODEL), lambda i, be, nb: (i, 0)),
        scratch_shapes=[pltpu.VMEM((2, MOE_ROWS, D_MODEL), F32),
                        pltpu.SemaphoreType.DMA((2,))])
    return pl.pallas_call(
        _expert_kernel,
        grid_spec=grid_spec,
        out_shape=jax.ShapeDtypeStruct((slots, D_MODEL), F32),
        compiler_params=_cparams(("arbitrary",), VMEM_BIG),
        name="experts",
    )(block_e, n_real, tok3, tok3, h, slot_gate, wi, bi, wo, bo)


def _moe(x, gain, w_router, rbias, wi, bi, wo, bo):
    n = x.shape[0]
    hb, idx_pad, gate_pad, rank_pad, cnt_pad = _router(x, gain, w_router, rbias)
    n_assign = n * TOP_K
    flat_e = idx_pad[:, :TOP_K].reshape(-1)
    flat_g = gate_pad[:, :TOP_K].reshape(-1)
    flat_rank = rank_pad[:, :TOP_K].reshape(-1)
    counts = cnt_pad[0, :N_EXPERTS]
    starts = jnp.cumsum(counts) - counts
    padded = (counts + MOE_ROWS - 1) // MOE_ROWS * MOE_ROWS
    pad_ends = jnp.cumsum(padded)
    pad_starts = pad_ends - padded
    n_blocks = -(-n_assign // MOE_ROWS) + N_EXPERTS
    slots = n_blocks * MOE_ROWS
    block_start = jnp.arange(n_blocks, dtype=jnp.int32) * MOE_ROWS
    block_e = jnp.minimum(jnp.sum(pad_ends[None, :] <= block_start[:, None], axis=1),
                          N_EXPERTS - 1).astype(jnp.int32)
    n_real = (pad_ends[-1] // MOE_ROWS).astype(jnp.int32).reshape(1)
    dest = pad_starts[flat_e] + flat_rank
    _, order = lax.sort_key_val(flat_e, jnp.arange(n_assign, dtype=jnp.int32))
    row_in_block = jnp.arange(MOE_ROWS, dtype=jnp.int32)[None, :]
    within = (block_start - pad_starts[block_e])[:, None] + row_in_block
    valid = (within < counts[block_e][:, None]).reshape(slots)
    src = jnp.clip(starts[block_e][:, None] + within, 0, n_assign - 1).reshape(slots)
    assign = order[src]
    slot_tok = jnp.where(valid, assign // TOP_K, 0)
    slot_gate = jnp.where(valid, flat_g[assign], 0.0)
    yb = _experts(block_e, n_real, slot_tok.astype(jnp.int32), hb,
                  slot_gate.reshape(slots, 1), wi, bi, wo, bo)
    y4 = yb[dest].reshape(n, TOP_K * D_MODEL)
    return _combine(x, y4)


def _block_diag(w):
    hh, d, _ = w.shape
    eye = jnp.eye(hh, dtype=w.dtype)
    return jnp.einsum('hij,hg->higj', w, eye).reshape(hh * d, hh * d)


def kernel(x_prompt, x_sample, cache_sb_k, cache_sb_v, page_table, state_lru_h, state_lru_conv, state_sconv, cache_mem_k, cache_mem_v, mem_prompt, ln_mix, w_in, gmlp_norm, gmlp_ws, gmlp_bs, lru_conv_w, lru_conv_b, lru_wa, lru_ba, lru_wx, lru_bx, lru_lambda, sb_q_norm, sb_k_norm, sb_bias, sconv_w, out_norm, w_out, ln_mem, ln_mem_kv, mem_wq, mem_wk, mem_wv, mem_q_norm, mem_k_norm, mem_wo, ln_moe, router_w, router_b, moe_w_in, moe_b_in, moe_w_out, moe_b_out):
    depth = w_in.shape[0]
    bp, sp, _ = x_prompt.shape
    bs, ss, _ = x_sample.shape
    n_p = bp * sp
    n_s = bs * ss
    n_pool = cache_sb_k.shape[1]

    x = jnp.concatenate([x_prompt.reshape(n_p, D_MODEL), x_sample.reshape(n_s, D_MODEL)], axis=0)
    mem2d = mem_prompt.reshape(bp * N_MEM, D_MODEL)
    pool_k = cache_sb_k.reshape(depth * n_pool, PAGE, GROUP_W)
    pool_v = cache_sb_v.reshape(depth * n_pool, PAGE, GROUP_W)
    cmem_k = cache_mem_k.reshape(depth * bs, N_MEM, MEM_W)
    cmem_v = cache_mem_v.reshape(depth * bs, N_MEM, MEM_W)

    w_in_b = w_in.astype(BF16)
    w_out_b = w_out.astype(BF16)
    mem_wq_b = mem_wq.astype(BF16)
    mem_wkv_b = jnp.concatenate([mem_wk, mem_wv], axis=2).astype(BF16)
    mem_wo_b = mem_wo.astype(BF16)
    moe_wi_b = moe_w_in.astype(BF16)
    moe_wo_b = moe_w_out.astype(BF16)
    rw_b = jnp.pad(router_w, ((0, 0), (0, 0), (0, LANES - N_EXPERTS))).astype(BF16)
    rb_pad = jnp.pad(router_b, ((0, 0), (0, LANES - N_EXPERTS))).reshape(depth, 1, LANES)

    zeros_h = jnp.zeros((bp, 1, GROUP_W), F32)
    zeros_lb = jnp.zeros((bp, 3, GROUP_W), F32)
    zeros_sb = jnp.zeros((bp, 2, GROUP_W), F32)

    outs = [[] for _ in range(14)]
    for l in range(depth):
        mkv = _norm_matmul(mem2d, ln_mem_kv[l], mem_wkv_b[l],
                           head_gain=jnp.concatenate([jnp.tile(mem_k_norm[l], MEM_HEADS),
                                                      jnp.ones((MEM_W,), F32)]).reshape(1, 2 * MEM_W),
                           norm_cols=MEM_W, head_dim=MEM_HEAD_DIM)
        mk_p = mkv[:, :MEM_W].reshape(bp, N_MEM, MEM_W)
        mv_p = mkv[:, MEM_W:].reshape(bp, N_MEM, MEM_W)

        p_all = _norm_matmul(x, ln_mix[l], w_in_b[l])
        common = dict(
            gng=gmlp_norm[l].reshape(1, GROUP_W),
            cw=lru_conv_w[l], cb=lru_conv_b[l].reshape(1, GROUP_W),
            wa=_block_diag(lru_wa[l]).astype(BF16), ba=lru_ba[l].reshape(1, GROUP_W),
            wx=_block_diag(lru_wx[l]).astype(BF16), bx=lru_bx[l].reshape(1, GROUP_W),
            lam=lru_lambda[l].reshape(1, GROUP_W),
            qg=jnp.tile(sb_q_norm[l], N_HEADS).reshape(1, GROUP_W),
            kg=jnp.tile(sb_k_norm[l], N_HEADS).reshape(1, GROUP_W),
            sw=sconv_w[l])
        lp = min(CHUNK, sp)
        ls = min(CHUNK, ss)
        wts_p = dict(common, ws=gmlp_ws[l][:, :lp, :lp],
                     bsf=jnp.repeat(gmlp_bs[l][:, :lp].T, HEAD_DIM, axis=1))
        ws_exp = jnp.repeat(jnp.transpose(gmlp_ws[l][:, :ls, :ls], (2, 1, 0)), HEAD_DIM, axis=2)
        wts_s = dict(common, ws=ws_exp, bsf=jnp.repeat(gmlp_bs[l][:, :ls].T, HEAD_DIM, axis=1))
        (ya_p, yb_p, yd_p, q_p, k_p, hl_p, lb_p, sbuf_p, gv_p) = _mixer_pre(
            p_all, 0, bp, sp, lp, zeros_h, zeros_lb, zeros_sb, wts_p)
        (ya_s, yb_s, yd_s, q_s, k_s, hl_s, lb_s, sbuf_s, gv_s) = _mixer_pre(
            p_all, n_p, bs, ss, ls, state_lru_h[l].reshape(bs, 1, GROUP_W), state_lru_conv[l],
            state_sconv[l], wts_s)
        yc_p = _sb_prompt(sb_bias[l], q_p, k_p, p_all, bp, sp)
        yc_s = _sb_sample(page_table + l * n_pool, sb_bias[l], q_s, k_s, p_all, n_p,
                          pool_k, pool_v, bs, ss)
        cat = lambda a, b: jnp.concatenate([a, b], axis=0)
        x = _merge(cat(ya_p, ya_s), cat(yb_p, yb_s), cat(yc_p, yc_s), cat(yd_p, yd_s),
                   out_norm[l], w_out_b[l], x)

        q_mem = _norm_matmul(x, ln_mem[l], mem_wq_b[l],
                             head_gain=jnp.tile(mem_q_norm[l], MEM_HEADS).reshape(1, MEM_W),
                             norm_cols=MEM_W, head_dim=MEM_HEAD_DIM)
        a_p = _mem_attn(q_mem, 0, bp, sp, 512, mk_p, mv_p, 0)
        a_s = _mem_attn(q_mem, n_p, bs, ss, ss, cmem_k, cmem_v, l * bs)
        x = _matmul_res(cat(a_p, a_s), mem_wo_b[l], x)

        x = _moe(x, ln_moe[l].reshape(1, D_MODEL), rw_b[l], rb_pad[l],
                 moe_wi_b[l], moe_b_in[l].reshape(N_EXPERTS, 1, 2 * D_EXPERT),
                 moe_wo_b[l], moe_b_out[l].reshape(N_EXPERTS, 1, D_MODEL))

        v_all = p_all[:, 1536:1792]
        outs[0].append(k_p.reshape(bp, sp, N_HEADS, HEAD_DIM))
        outs[1].append(v_all[:n_p].reshape(bp, sp, N_HEADS, HEAD_DIM))
        outs[2].append(k_s.reshape(bs, ss, N_HEADS, HEAD_DIM))
        outs[3].append(v_all[n_p:].reshape(bs, ss, N_HEADS, HEAD_DIM))
        outs[4].append(hl_p.reshape(bp, GROUP_W))
        outs[5].append(hl_s.reshape(bs, GROUP_W))
        outs[6].append(lb_p)
        outs[7].append(lb_s)
        outs[8].append(sbuf_p)
        outs[9].append(sbuf_s)
        outs[10].append(gv_p.reshape(bp, lp, N_HEADS, HEAD_DIM))
        outs[11].append(gv_s.reshape(bs, ls, N_HEADS, HEAD_DIM))
        outs[12].append(mk_p.reshape(bp, N_MEM, MEM_HEADS, MEM_HEAD_DIM))
        outs[13].append(mv_p.reshape(bp, N_MEM, MEM_HEADS, MEM_HEAD_DIM))

    y_prompt = x[:n_p].reshape(bp, sp, D_MODEL)
    y_sample = x[n_p:].reshape(bs, ss, D_MODEL)
    return (y_prompt, y_sample) + tuple(jnp.stack(o) for o in outs)
```

```python
import functools
import math

import jax
import jax.numpy as jnp
from jax import lax
from jax.experimental import pallas as pl
from jax.experimental.pallas import tpu as pltpu

F32 = jnp.float32
BF16 = jnp.bfloat16

D_MODEL = 1024
GROUP_W = 256
N_HEADS = 4
HEAD_DIM = 64
HEAD_SHIFT = 6
PROJ_W = 10 * GROUP_W
CHUNK = 128
LRU_C = 8.0
N_MEM = 256
MEM_HEADS = 4
MEM_HEAD_DIM = 128
MEM_W = MEM_HEADS * MEM_HEAD_DIM
N_EXPERTS = 32
TOP_K = 4
D_EXPERT = 1024
SWIGLU_LIMIT = 7.0
SWIGLU_ALPHA = 1.702
RMS_EPS = 1e-6
PAGE = 128
LANES = 128

ROW_TILE = 512
MOE_ROWS = 512
MOE_HCHUNK = 512
VMEM_BIG = 48 * 1024 * 1024
VMEM_EXPERTS = 56 * 1024 * 1024


def _cparams(sem, vmem=None):
    return pltpu.CompilerParams(dimension_semantics=sem, vmem_limit_bytes=vmem)


def _bdot(a, b):
    return jnp.dot(a.astype(BF16), b.astype(BF16), preferred_element_type=F32)


def _gelu(x):
    return 0.5 * x * (1.0 + jnp.tanh(0.7978845608028654 * (x + 0.044715 * (x * x * x))))


def _rms(x, gain):
    return x * lax.rsqrt(jnp.mean(x * x, axis=-1, keepdims=True) + RMS_EPS) * gain


def _lane_head():
    return lax.broadcasted_iota(jnp.int32, (1, GROUP_W), 1) >> HEAD_SHIFT


def _split_dot(s, ones_bf16):
    hi = s.astype(BF16)
    lo = (s - hi.astype(F32)).astype(BF16)
    return (jnp.dot(hi, ones_bf16, preferred_element_type=F32)
            + jnp.dot(lo, ones_bf16, preferred_element_type=F32))


def _norm_matmul_kernel(x_ref, g_ref, w_ref, hg_ref, o_ref, *, n_chunk, norm_cols, head_dim):
    h = _rms(x_ref[...], g_ref[...]).astype(BF16)
    n = o_ref.shape[1]
    for c0 in range(0, n, n_chunk):
        y = jnp.dot(h, w_ref[:, c0:c0 + n_chunk], preferred_element_type=F32)
        for h0 in range(0, n_chunk, head_dim):
            col = c0 + h0
            if col < norm_cols:
                yh = y[:, h0:h0 + head_dim]
                o_ref[:, col:col + head_dim] = _rms(yh, hg_ref[:, col:col + head_dim])
        if c0 + n_chunk > norm_cols:
            lo = max(c0, norm_cols)
            o_ref[:, lo:c0 + n_chunk] = y[:, lo - c0:]


def _norm_matmul(x, gain, w_bf16, head_gain=None, norm_cols=0, head_dim=LANES, n_chunk=512):
    m, k = x.shape
    n = w_bf16.shape[1]
    n_chunk = min(n_chunk, n)
    if head_gain is None:
        head_gain = jnp.ones((1, n), F32)
    kern = functools.partial(_norm_matmul_kernel, n_chunk=n_chunk, norm_cols=norm_cols,
                             head_dim=head_dim)
    return pl.pallas_call(
        kern,
        grid=(m // ROW_TILE,),
        in_specs=[pl.BlockSpec((ROW_TILE, k), lambda i: (i, 0)),
                  pl.BlockSpec((1, k), lambda i: (0, 0)),
                  pl.BlockSpec((k, n), lambda i: (0, 0)),
                  pl.BlockSpec((1, n), lambda i: (0, 0))],
        out_specs=pl.BlockSpec((ROW_TILE, n), lambda i: (i, 0)),
        out_shape=jax.ShapeDtypeStruct((m, n), F32),
        compiler_params=_cparams(("parallel",), VMEM_BIG),
        name="norm_matmul",
    )(x, gain.reshape(1, k), w_bf16, head_gain)


def _matmul_res_kernel(a_ref, w_ref, x_ref, o_ref):
    o_ref[...] = x_ref[...] + jnp.dot(a_ref[...].astype(BF16), w_ref[...],
                                      preferred_element_type=F32)


def _matmul_res(a, w_bf16, x):
    m, k = a.shape
    n = w_bf16.shape[1]
    return pl.pallas_call(
        _matmul_res_kernel,
        grid=(m // ROW_TILE,),
        in_specs=[pl.BlockSpec((ROW_TILE, k), lambda i: (i, 0)),
                  pl.BlockSpec((k, n), lambda i: (0, 0)),
                  pl.BlockSpec((ROW_TILE, n), lambda i: (i, 0))],
        out_specs=pl.BlockSpec((ROW_TILE, n), lambda i: (i, 0)),
        out_shape=jax.ShapeDtypeStruct((m, n), F32),
        compiler_params=_cparams(("parallel",), VMEM_BIG),
        name="matmul_res",
    )(a, w_bf16, x)


def _merge_kernel(ya_ref, yb_ref, yc_ref, yd_ref, g_ref, w_ref, x_ref, o_ref):
    parts = []
    for gi, y_ref in enumerate((ya_ref, yb_ref, yc_ref, yd_ref)):
        parts.append(_rms(y_ref[...], g_ref[gi:gi + 1, :]).astype(BF16))
    y = jnp.concatenate(parts, axis=1)
    o_ref[...] = x_ref[...] + jnp.dot(y, w_ref[...], preferred_element_type=F32)


def _merge(ya, yb, yc, yd, out_gain, w_bf16, x):
    m = x.shape[0]
    yspec = pl.BlockSpec((ROW_TILE, GROUP_W), lambda i: (i, 0))
    return pl.pallas_call(
        _merge_kernel,
        grid=(m // ROW_TILE,),
        in_specs=[yspec, yspec, yspec, yspec,
                  pl.BlockSpec((4, GROUP_W), lambda i: (0, 0)),
                  pl.BlockSpec((D_MODEL, D_MODEL), lambda i: (0, 0)),
                  pl.BlockSpec((ROW_TILE, D_MODEL), lambda i: (i, 0))],
        out_specs=pl.BlockSpec((ROW_TILE, D_MODEL), lambda i: (i, 0)),
        out_shape=jax.ShapeDtypeStruct((m, D_MODEL), F32),
        compiler_params=_cparams(("parallel",), VMEM_BIG),
        name="merge",
    )(ya, yb, yc, yd, out_gain, w_bf16, x)


def _shift_rows(x, s, fill):
    rolled = pltpu.roll(x, s, axis=0)
    row = lax.broadcasted_iota(jnp.int32, x.shape, 0)
    return jnp.where(row >= s, rolled, fill)


def _mixer_kernel(p_ref, h0_ref, lbuf_ref, sbuf_ref, gng_ref, ws_ref, bsf_ref, cw_ref, cb_ref,
                  wa_ref, ba_ref, wx_ref, bx_ref, lam_ref, qg_ref, kg_ref, sw_ref,
                  ya_ref, yb_ref, yd_ref, q_ref, k_ref, hl_ref, lbo_ref, sbo_ref, gv_ref,
                  xb_ref, eb_ref, h_ref, *, tb):
    t = pl.program_id(1)

    @pl.when(t == 0)
    def _():
        xb_ref[5:8, :] = lbuf_ref[0]
        eb_ref[6:8, :] = sbuf_ref[0]
        h_ref[...] = h0_ref[0]

    lane_head = _lane_head()

    a_u = p_ref[:, 0:256]
    a_v = p_ref[:, 256:512]
    v2 = _rms(_gelu(a_v), gng_ref[...])
    gv_ref[0] = v2
    mixed = bsf_ref[...]
    if tb == CHUNK:
        row = lax.broadcasted_iota(jnp.int32, (tb, tb), 0)
        col = lax.broadcasted_iota(jnp.int32, (tb, tb), 1)
        for hd in range(N_HEADS):
            w_h = jnp.where(row >= col, ws_ref[hd], 0.0)
            mixed = mixed + _bdot(w_h, jnp.where(lane_head == hd, v2, 0.0))
    else:
        row = lax.broadcasted_iota(jnp.int32, (tb, GROUP_W), 0)
        v2r = v2.astype(BF16).astype(F32)
        for s in range(tb):
            w_s = jnp.where(row >= s, ws_ref[s], 0.0).astype(BF16).astype(F32)
            mixed = mixed + w_s * v2r[s:s + 1, :]
    ya_ref[...] = _gelu(a_u) * mixed

    b_x = p_ref[:, 512:768]
    b_g = p_ref[:, 768:1024]
    xb_ref[8:8 + tb, :] = b_x
    conv = xb_ref[5:5 + tb, :] * cw_ref[0:1, :]
    for kk in range(1, 4):
        conv = conv + xb_ref[5 + kk:5 + kk + tb, :] * cw_ref[kk:kk + 1, :]
    tail = xb_ref[tb + 5:tb + 8, :]
    lbo_ref[0] = tail
    xb_ref[5:8, :] = tail
    xc = conv + cb_ref[...]
    xcb = xc.astype(BF16)
    r = jax.nn.sigmoid(jnp.dot(xcb, wa_ref[...], preferred_element_type=F32) + ba_ref[...])
    ig = jax.nn.sigmoid(jnp.dot(xcb, wx_ref[...], preferred_element_type=F32) + bx_ref[...])
    lam = lam_ref[...]
    log_sig_lam = -(jnp.maximum(-lam, 0.0) + jnp.log1p(jnp.exp(-jnp.abs(lam))))
    log_a = (LRU_C * r) * log_sig_lam
    a = jnp.exp(log_a)
    one_minus_a2 = -jnp.tanh(log_a) * (a * a + 1.0)
    bb = jnp.sqrt(one_minus_a2) * (ig * xc)
    aa = a
    s = 1
    while s < tb:
        a_sh = _shift_rows(aa, s, 1.0)
        b_sh = _shift_rows(bb, s, 0.0)
        bb = aa * b_sh + bb
        aa = aa * a_sh
        s *= 2
    hseq = aa * h_ref[...] + bb
    h_last = hseq[tb - 1:tb, :]
    h_ref[...] = h_last
    hl_ref[0] = h_last
    yb_ref[...] = hseq * _gelu(b_g)

    ones_bd = jnp.where(
        (lax.broadcasted_iota(jnp.int32, (GROUP_W, GROUP_W), 0) >> HEAD_SHIFT)
        == (lax.broadcasted_iota(jnp.int32, (GROUP_W, GROUP_W), 1) >> HEAD_SHIFT),
        1.0, 0.0).astype(BF16)
    for src, gain_ref, dst in ((1024, qg_ref, q_ref), (1280, kg_ref, k_ref)):
        xq = p_ref[:, src:src + 256]
        ms = _split_dot(xq * xq, ones_bd) * (1.0 / HEAD_DIM)
        dst[...] = xq * lax.rsqrt(ms + RMS_EPS) * gain_ref[...]

    d_b = p_ref[:, 1792:2048]
    e = p_ref[:, 2048:2304] * p_ref[:, 2304:2560]
    eb_ref[8:8 + tb, :] = e
    dconv = eb_ref[6:6 + tb, :] * sw_ref[0:1, :]
    for kk in range(1, 3):
        dconv = dconv + eb_ref[6 + kk:6 + kk + tb, :] * sw_ref[kk:kk + 1, :]
    etail = eb_ref[tb + 6:tb + 8, :]
    sbo_ref[0] = etail
    eb_ref[6:8, :] = etail
    yd_ref[...] = d_b * dconv


def _mixer_pre(p_all, row0, batch, seq, tb, h0, lbuf, sbuf, wts):
    nt = seq // tb
    rb0 = row0 // tb
    rows = batch * seq
    rmap = lambda b, t: (b * nt + t, 0)
    cmap2 = lambda b, t: (0, 0)
    cmap3 = lambda b, t: (0, 0, 0)
    smap = lambda b, t: (b, 0, 0)
    yspec = pl.BlockSpec((tb, GROUP_W), rmap)
    vec = pl.BlockSpec((1, GROUP_W), cmap2)
    wsq = pl.BlockSpec((GROUP_W, GROUP_W), cmap2)
    ws = wts["ws"]
    y_sds = jax.ShapeDtypeStruct((rows, GROUP_W), F32)
    kern = functools.partial(_mixer_kernel, tb=tb)
    return pl.pallas_call(
        kern,
        grid=(batch, nt),
        in_specs=[pl.BlockSpec((tb, PROJ_W), lambda b, t: (rb0 + b * nt + t, 0)),
                  pl.BlockSpec((1, 1, GROUP_W), smap),
                  pl.BlockSpec((1, 3, GROUP_W), smap),
                  pl.BlockSpec((1, 2, GROUP_W), smap),
                  vec,
                  pl.BlockSpec(ws.shape, cmap3),
                  pl.BlockSpec((tb, GROUP_W), cmap2),
                  pl.BlockSpec((4, GROUP_W), cmap2), vec,
                  wsq, vec, wsq, vec, vec, vec, vec,
                  pl.BlockSpec((3, GROUP_W), cmap2)],
        out_specs=[yspec, yspec, yspec, yspec, yspec,
                   pl.BlockSpec((1, 1, GROUP_W), smap),
                   pl.BlockSpec((1, 3, GROUP_W), smap),
                   pl.BlockSpec((1, 2, GROUP_W), smap),
                   pl.BlockSpec((1, tb, GROUP_W), smap)],
        out_shape=[y_sds, y_sds, y_sds, y_sds, y_sds,
                   jax.ShapeDtypeStruct((batch, 1, GROUP_W), F32),
                   jax.ShapeDtypeStruct((batch, 3, GROUP_W), F32),
                   jax.ShapeDtypeStruct((batch, 2, GROUP_W), F32),
                   jax.ShapeDtypeStruct((batch, tb, GROUP_W), F32)],
        scratch_shapes=[pltpu.VMEM((tb + 8, GROUP_W), F32),
                        pltpu.VMEM((tb + 8, GROUP_W), F32),
                        pltpu.VMEM((1, GROUP_W), F32)],
        compiler_params=_cparams(("arbitrary", "arbitrary")),
        name="mixer_pre",
    )(p_all, h0, lbuf, sbuf, wts["gng"], ws, wts["bsf"], wts["cw"], wts["cb"],
      wts["wa"], wts["ba"], wts["wx"], wts["bx"], wts["lam"], wts["qg"], wts["kg"], wts["sw"])


def _sb_setup(q, bias_ref, rows_per_head):
    nr = N_HEADS * rows_per_head
    lane_head = _lane_head()
    qs = jnp.concatenate([jnp.where(lane_head == hd, q, 0.0) for hd in range(N_HEADS)],
                         axis=0).astype(BF16)
    row_head = lax.broadcasted_iota(jnp.int32, (nr, LANES), 0) >> int(math.log2(rows_per_head))
    bias = jnp.full((nr, LANES), bias_ref[N_HEADS - 1], F32)
    for hd in range(N_HEADS - 2, -1, -1):
        bias = jnp.where(row_head == hd, bias_ref[hd], bias)
    jj = lax.broadcasted_iota(jnp.int32, (LANES, 2 * LANES), 0)
    ss = lax.broadcasted_iota(jnp.int32, (LANES, 2 * LANES), 1)
    cum = jnp.where((jj > ss) | (ss >= LANES), 1.0, 0.0).astype(BF16)
    return qs, bias, cum


def _sb_tile(qs, bias, cum, k, v, acc, c, mask):
    z = lax.dot_general(qs, k.astype(BF16), (((1,), (1,)), ((), ())),
                        preferred_element_type=F32) + bias
    lk = -(jnp.maximum(z, 0.0) + jnp.log1p(jnp.exp(-jnp.abs(z))))
    lkm = lk if mask is None else jnp.where(mask, lk, 0.0)
    r = _split_dot(lkm, cum)
    w = jnp.exp(z + lk + r[:, :LANES] + c)
    if mask is not None:
        w = jnp.where(mask, w, 0.0)
    acc = acc + jnp.dot(w.astype(BF16), v.astype(BF16), preferred_element_type=F32)
    return acc, c + r[:, LANES:]


def _sb_tile_ref(qs, bias, cum, k, v, acc_ref, c_ref, mask):
    acc, c = _sb_tile(qs, bias, cum, k, v, acc_ref[...], c_ref[...], mask)
    acc_ref[...] = acc
    c_ref[...] = c


def _sb_finish(acc, o_ref, rows_per_head):
    lane_head = _lane_head()
    out = jnp.zeros((rows_per_head, GROUP_W), F32)
    for hd in range(N_HEADS):
        out = out + jnp.where(lane_head == hd,
                              acc[hd * rows_per_head:(hd + 1) * rows_per_head, :], 0.0)
    o_ref[...] = out


def _sb_prompt_kernel(bias_ref, q_ref, k_ref, v_ref, o_ref, acc_ref, c_ref):
    i = pl.program_id(1)
    nr = N_HEADS * CHUNK
    qs, bias, cum = _sb_setup(q_ref[...] * (HEAD_DIM ** -0.5), bias_ref, CHUNK)
    acc_ref[...] = jnp.zeros_like(acc_ref)
    c_ref[...] = jnp.zeros_like(c_ref)
    tq = lax.broadcasted_iota(jnp.int32, (nr, LANES), 0) & (CHUNK - 1)
    ts = lax.broadcasted_iota(jnp.int32, (nr, LANES), 1)
    d0 = pl.multiple_of(i * CHUNK, CHUNK)
    _sb_tile_ref(qs, bias, cum, k_ref[pl.ds(d0, CHUNK), :], v_ref[pl.ds(d0, CHUNK), :],
                 acc_ref, c_ref, ts < tq)

    odd = i & 1

    @pl.when(odd == 1)
    def _():
        k0 = pl.multiple_of((i - 1) * CHUNK, CHUNK)
        _sb_tile_ref(qs, bias, cum, k_ref[pl.ds(k0, CHUNK), :], v_ref[pl.ds(k0, CHUNK), :],
                     acc_ref, c_ref, None)

    def body(j, carry):
        ka = pl.multiple_of((i - odd - 1 - 2 * j) * CHUNK, CHUNK)
        kb = pl.multiple_of((i - odd - 2 - 2 * j) * CHUNK, CHUNK)
        acc, c = _sb_tile(qs, bias, cum, k_ref[pl.ds(ka, CHUNK), :], v_ref[pl.ds(ka, CHUNK), :],
                          acc_ref[...], c_ref[...], None)
        acc, c = _sb_tile(qs, bias, cum, k_ref[pl.ds(kb, CHUNK), :], v_ref[pl.ds(kb, CHUNK), :],
                          acc, c, None)
        acc_ref[...] = acc
        c_ref[...] = c
        return carry

    lax.fori_loop(0, i >> 1, body, 0)
    _sb_finish(acc_ref[...], o_ref, CHUNK)


def _sb_prompt(sb_bias, q_p, k_p, p_all, batch, seq):
    nq = seq // CHUNK
    return pl.pallas_call(
        _sb_prompt_kernel,
        grid=(batch, nq),
        in_specs=[pl.BlockSpec(memory_space=pltpu.SMEM),
                  pl.BlockSpec((CHUNK, GROUP_W), lambda b, i: (b * nq + i, 0)),
                  pl.BlockSpec((seq, GROUP_W), lambda b, i: (b, 0)),
                  pl.BlockSpec((seq, GROUP_W), lambda b, i: (b, 6))],
        out_specs=pl.BlockSpec((CHUNK, GROUP_W), lambda b, i: (b * nq + i, 0)),
        out_shape=jax.ShapeDtypeStruct((batch * seq, GROUP_W), F32),
        scratch_shapes=[pltpu.VMEM((N_HEADS * CHUNK, GROUP_W), F32),
                        pltpu.VMEM((N_HEADS * CHUNK, LANES), F32)],
        compiler_params=_cparams(("parallel", "arbitrary")),
        name="sb_prompt",
    )(sb_bias, q_p, k_p, p_all)


def _sb_sample_kernel(pt_ref, bias_ref, q_ref, kn_ref, vn_ref, *rest, tq, n_pages):
    kp_refs = rest[:n_pages]
    vp_refs = rest[n_pages:2 * n_pages]
    o_ref, kpad_ref, vpad_ref = rest[2 * n_pages:]
    nr = N_HEADS * tq
    qs, bias, cum = _sb_setup(q_ref[...] * (HEAD_DIM ** -0.5), bias_ref, tq)
    kpad_ref[...] = jnp.zeros_like(kpad_ref)
    vpad_ref[...] = jnp.zeros_like(vpad_ref)
    kpad_ref[0:tq, :] = kn_ref[...]
    vpad_ref[0:tq, :] = vn_ref[...]
    trow = lax.broadcasted_iota(jnp.int32, (nr, LANES), 0) & (tq - 1)
    ts = lax.broadcasted_iota(jnp.int32, (nr, LANES), 1)
    acc = jnp.zeros((nr, GROUP_W), F32)
    c = jnp.zeros((nr, LANES), F32)
    acc, c = _sb_tile(qs, bias, cum, kpad_ref[...], vpad_ref[...], acc, c, ts < trow)
    for j in range(n_pages - 1, -1, -1):
        acc, c = _sb_tile(qs, bias, cum, kp_refs[j][0], vp_refs[j][0], acc, c, None)
    _sb_finish(acc, o_ref, tq)


def _sb_sample(page_rows, sb_bias, q_s, k_s, p_all, row0, pool_k, pool_v, batch, tq):
    n_pages = page_rows.shape[1]
    page_rows = page_rows.reshape(-1)
    rb0 = row0 // tq

    def page_spec(j):
        return pl.BlockSpec((1, PAGE, GROUP_W), lambda b, pt: (pt[b * n_pages + j], 0, 0))

    kern = functools.partial(_sb_sample_kernel, tq=tq, n_pages=n_pages)
    grid_spec = pltpu.PrefetchScalarGridSpec(
        num_scalar_prefetch=1,
        grid=(batch,),
        in_specs=([pl.BlockSpec(memory_space=pltpu.SMEM),
                   pl.BlockSpec((tq, GROUP_W), lambda b, pt: (b, 0)),
                   pl.BlockSpec((tq, GROUP_W), lambda b, pt: (b, 0)),
                   pl.BlockSpec((tq, GROUP_W), lambda b, pt: (rb0 + b, 6))]
                  + [page_spec(j) for j in range(n_pages)]
                  + [page_spec(j) for j in range(n_pages)]),
        out_specs=pl.BlockSpec((tq, GROUP_W), lambda b, pt: (b, 0)),
        scratch_shapes=[pltpu.VMEM((PAGE, GROUP_W), F32),
                        pltpu.VMEM((PAGE, GROUP_W), F32)])
    return pl.pallas_call(
        kern,
        grid_spec=grid_spec,
        out_shape=jax.ShapeDtypeStruct((batch * tq, GROUP_W), F32),
        compiler_params=_cparams(("parallel",)),
        name="sb_sample",
    )(page_rows, sb_bias, q_s, k_s, p_all, *([pool_k] * n_pages), *([pool_v] * n_pages))


def _mem_attn_kernel(q_ref, k_ref, v_ref, o_ref):
    scale = MEM_HEAD_DIM ** -0.5
    for hd in range(MEM_HEADS):
        sl = slice(hd * MEM_HEAD_DIM, (hd + 1) * MEM_HEAD_DIM)
        qh = q_ref[:, sl].astype(BF16)
        kh = k_ref[0, :, sl].astype(BF16)
        s = lax.dot_general(qh, kh, (((1,), (1,)), ((), ())), preferred_element_type=F32) * scale
        s = s - jnp.max(s, axis=-1, keepdims=True)
        e = jnp.exp(s)
        p = e / jnp.sum(e, axis=-1, keepdims=True)
        o_ref[:, sl] = jnp.dot(p.astype(BF16), v_ref[0, :, sl].astype(BF16),
                               preferred_element_type=F32)


def _mem_attn(q_all, row0, batch, seq, tq, mem_k, mem_v, kv0):
    nt = seq // tq
    rb0 = row0 // tq
    kvspec = pl.BlockSpec((1, N_MEM, MEM_W), lambda b, t: (kv0 + b, 0, 0))
    return pl.pallas_call(
        _mem_attn_kernel,
        grid=(batch, nt),
        in_specs=[pl.BlockSpec((tq, MEM_W), lambda b, t: (rb0 + b * nt + t, 0)), kvspec, kvspec],
        out_specs=pl.BlockSpec((tq, MEM_W), lambda b, t: (b * nt + t, 0)),
        out_shape=jax.ShapeDtypeStruct((batch * seq, MEM_W), F32),
        compiler_params=_cparams(("parallel", "arbitrary")),
        name="mem_attn",
    )(q_all, mem_k, mem_v)


def _router_kernel(x_ref, g_ref, w_ref, b_ref, h_ref, idx_ref, gate_ref, rank_ref, cnt_ref,
                   run_ref):
    i = pl.program_id(0)

    @pl.when(i == 0)
    def _():
        run_ref[...] = jnp.zeros_like(run_ref)

    hb = _rms(x_ref[...], g_ref[...]).astype(BF16)
    h_ref[...] = hb.astype(F32)
    logits = jnp.dot(hb, w_ref[...], preferred_element_type=F32) + b_ref[...]
    tm = logits.shape[0]
    lane = lax.broadcasted_iota(jnp.int32, logits.shape, 1).astype(F32)
    vals = jnp.where(lane < N_EXPERTS, logits, -jnp.inf)
    idx_out = jnp.zeros(logits.shape, F32)
    e_out = jnp.zeros(logits.shape, F32)
    picked = jnp.zeros(logits.shape, F32)
    sels = []
    m0 = None
    denom = None
    for kk in range(TOP_K):
        m = jnp.max(vals, axis=-1, keepdims=True)
        idx = jnp.min(jnp.where(vals == m, lane, float(LANES)), axis=-1, keepdims=True)
        sel = lane == idx
        sels.append(sel)
        picked = jnp.where(sel, 1.0, picked)
        vals = jnp.where(sel, -jnp.inf, vals)
        if kk == 0:
            m0 = m
        e = jnp.exp(m - m0)
        denom = e if kk == 0 else denom + e
        idx_out = jnp.where(lane == kk, idx, idx_out)
        e_out = jnp.where(lane == kk, e, e_out)
    idx_ref[...] = idx_out.astype(jnp.int32)
    gate_ref[...] = e_out / denom

    row = lax.broadcasted_iota(jnp.int32, (tm, tm), 0)
    col = lax.broadcasted_iota(jnp.int32, (tm, tm), 1)
    before = jnp.where(row > col, 1.0, 0.0).astype(BF16)
    prefix = jnp.dot(before, picked.astype(BF16), preferred_element_type=F32) + run_ref[...]
    rank_out = jnp.zeros(logits.shape, F32)
    for kk in range(TOP_K):
        rk = jnp.sum(jnp.where(sels[kk], prefix, 0.0), axis=-1, keepdims=True)
        rank_out = jnp.where(lane == kk, rk, rank_out)
    rank_ref[...] = rank_out.astype(jnp.int32)
    total = run_ref[...] + jnp.sum(picked, axis=0, keepdims=True)
    run_ref[...] = total
    cnt_ref[...] = total.astype(jnp.int32)


def _router(x, gain, w_bf16, bias):
    m = x.shape[0]
    rspec = pl.BlockSpec((ROW_TILE, LANES), lambda i: (i, 0))
    return pl.pallas_call(
        _router_kernel,
        grid=(m // ROW_TILE,),
        in_specs=[pl.BlockSpec((ROW_TILE, D_MODEL), lambda i: (i, 0)),
                  pl.BlockSpec((1, D_MODEL), lambda i: (0, 0)),
                  pl.BlockSpec((D_MODEL, LANES), lambda i: (0, 0)),
                  pl.BlockSpec((1, LANES), lambda i: (0, 0))],
        out_specs=[pl.BlockSpec((ROW_TILE, D_MODEL), lambda i: (i, 0)), rspec, rspec, rspec,
                   pl.BlockSpec((1, LANES), lambda i: (0, 0))],
        out_shape=[jax.ShapeDtypeStruct((m, D_MODEL), F32),
                   jax.ShapeDtypeStruct((m, LANES), jnp.int32),
                   jax.ShapeDtypeStruct((m, LANES), F32),
                   jax.ShapeDtypeStruct((m, LANES), jnp.int32),
                   jax.ShapeDtypeStruct((1, LANES), jnp.int32)],
        scratch_shapes=[pltpu.VMEM((1, LANES), F32)],
        compiler_params=_cparams(("arbitrary",)),
        name="router",
    )(x, gain, w_bf16, bias)


def _combine_kernel(x_ref, g_ref, y0_ref, y1_ref, y2_ref, y3_ref, o_ref):
    g = g_ref[...]
    y = ((y0_ref[...] * g[:, 0:1] + y1_ref[...] * g[:, 1:2])
         + (y2_ref[...] * g[:, 2:3] + y3_ref[...] * g[:, 3:4]))
    o_ref[...] = x_ref[...] + y


def _combine(x, gate_pad, yk):
    m, d = x.shape
    tm = 256
    nt = m // tm
    yspecs = [pl.BlockSpec((tm, d), lambda i, kk=kk: (kk * nt + i, 0)) for kk in range(TOP_K)]
    return pl.pallas_call(
        _combine_kernel,
        grid=(nt,),
        in_specs=[pl.BlockSpec((tm, d), lambda i: (i, 0)),
                  pl.BlockSpec((tm, LANES), lambda i: (i, 0))] + yspecs,
        out_specs=pl.BlockSpec((tm, d), lambda i: (i, 0)),
        out_shape=jax.ShapeDtypeStruct((m, d), F32),
        compiler_params=_cparams(("parallel",)),
        name="combine",
    )(x, gate_pad, yk, yk, yk, yk)


def _expert_kernel(be_ref, xoff_ref, nb_ref, xs_hbm, wi_ref, bi_ref, wo_ref, bo_ref, o_ref,
                   xbuf_ref, wib_ref, wob_ref, sem_ref):
    i = pl.program_id(0)
    nb = nb_ref[0]
    cur = i % 2

    def window_copy(step, buf):
        off = pl.multiple_of(xoff_ref[step], 8)
        return pltpu.make_async_copy(xs_hbm.at[pl.ds(off, MOE_ROWS), :], xbuf_ref.at[buf],
                                     sem_ref.at[buf])

    @pl.when(jnp.logical_and(i == 0, nb > 0))
    def _():
        window_copy(0, 0).start()

    @pl.when(i + 1 < nb)
    def _():
        window_copy(i + 1, 1 - cur).start()

    @pl.when(i < nb)
    def _():
        new_expert = jnp.logical_or(i == 0, be_ref[i] != be_ref[jnp.maximum(i - 1, 0)])

        @pl.when(new_expert)
        def _():
            rows = 64

            def cast_rows(r, carry):
                r0 = pl.multiple_of(r * rows, rows)
                wib_ref[pl.ds(r0, rows), :] = wi_ref[0, pl.ds(r0, rows), :].astype(BF16)
                wob_ref[pl.ds(r0, rows), :] = wo_ref[0, pl.ds(r0, rows), :].astype(BF16)
                return carry
            lax.fori_loop(0, D_MODEL // rows, cast_rows, 0)

        window_copy(i, cur).wait()
        x = xbuf_ref[cur].astype(BF16)
        acc = jnp.zeros(o_ref.shape, F32)
        for c0 in range(0, D_EXPERT, MOE_HCHUNK):
            g = jnp.dot(x, wib_ref[:, c0:c0 + MOE_HCHUNK], preferred_element_type=F32)
            g = g + bi_ref[0, :, c0:c0 + MOE_HCHUNK]
            u = jnp.dot(x, wib_ref[:, D_EXPERT + c0:D_EXPERT + c0 + MOE_HCHUNK],
                        preferred_element_type=F32)
            u = u + bi_ref[0, :, D_EXPERT + c0:D_EXPERT + c0 + MOE_HCHUNK]
            g = jnp.minimum(g, SWIGLU_LIMIT)
            u = jnp.clip(u, -SWIGLU_LIMIT, SWIGLU_LIMIT)
            act = g * jax.nn.sigmoid(SWIGLU_ALPHA * g) * (u + 1.0)
            acc = acc + jnp.dot(act.astype(BF16), wob_ref[c0:c0 + MOE_HCHUNK, :],
                                preferred_element_type=F32)
        o_ref[...] = acc + bo_ref[0]

    @pl.when(i >= nb)
    def _():
        o_ref[...] = jnp.zeros_like(o_ref)


def _experts(block_w, xoff, n_real, xs, wi, bi, wo, bo):
    n_blocks = block_w.shape[0]
    wmap = lambda i, bw, xo, nb: (bw[i], 0, 0)
    grid_spec = pltpu.PrefetchScalarGridSpec(
        num_scalar_prefetch=3,
        grid=(n_blocks,),
        in_specs=[pl.BlockSpec(memory_space=pl.ANY),
                  pl.BlockSpec((1, D_MODEL, 2 * D_EXPERT), wmap),
                  pl.BlockSpec((1, 1, 2 * D_EXPERT), wmap),
                  pl.BlockSpec((1, D_EXPERT, D_MODEL), wmap),
                  pl.BlockSpec((1, 1, D_MODEL), wmap)],
        out_specs=pl.BlockSpec((MOE_ROWS, D_MODEL), lambda i, bw, xo, nb: (i, 0)),
        scratch_shapes=[pltpu.VMEM((2, MOE_ROWS, D_MODEL), F32),
                        pltpu.VMEM((D_MODEL, 2 * D_EXPERT), BF16),
                        pltpu.VMEM((D_EXPERT, D_MODEL), BF16),
                        pltpu.SemaphoreType.DMA((2,))])
    return pl.pallas_call(
        _expert_kernel,
        grid_spec=grid_spec,
        out_shape=jax.ShapeDtypeStruct((n_blocks * MOE_ROWS, D_MODEL), F32),
        compiler_params=_cparams(("arbitrary",), VMEM_EXPERTS),
        name="experts",
    )(block_w, xoff, n_real, xs, wi, bi, wo, bo)


def _lookup(table, idx):
    experts = jnp.arange(N_EXPERTS, dtype=jnp.int32)
    return jnp.sum(jnp.where(idx[:, None] == experts[None, :], table[None, :], 0), axis=1)


def _moe(x, gain, w_router, rbias, wi, bi, wo, bo, layer):
    n = x.shape[0]
    h, idx_pad, gate_pad, rank_pad, cnt_pad = _router(x, gain, w_router, rbias)
    n_assign = n * TOP_K
    flat_e = idx_pad[:, :TOP_K].reshape(-1)
    flat_rank = rank_pad[:, :TOP_K].reshape(-1)
    counts = cnt_pad[0, :N_EXPERTS]
    starts = jnp.cumsum(counts) - counts
    padded = (counts + MOE_ROWS - 1) // MOE_ROWS * MOE_ROWS
    pad_ends = jnp.cumsum(padded)
    pad_starts = pad_ends - padded
    n_blocks = -(-n_assign // MOE_ROWS) + N_EXPERTS
    block_start = jnp.arange(n_blocks, dtype=jnp.int32) * MOE_ROWS
    block_e = jnp.minimum(jnp.sum(pad_ends[None, :] <= block_start[:, None], axis=1),
                          N_EXPERTS - 1).astype(jnp.int32)
    n_real = (pad_ends[-1] // MOE_ROWS).astype(jnp.int32).reshape(1)
    seg = (counts + 7) // 8 * 8
    seg_ends = jnp.cumsum(seg)
    seg_starts = seg_ends - seg
    n_in = n_assign + 8 * N_EXPERTS + MOE_ROWS
    row = jnp.arange(n_in, dtype=jnp.int32)
    shift = seg_starts - starts
    dshift = shift - jnp.concatenate([jnp.zeros((1,), jnp.int32), shift[:-1]])
    shift_row = jnp.sum(jnp.where(row[:, None] >= seg_starts[None, :], dshift[None, :], 0), axis=1)
    in_gap = jnp.any((row[:, None] >= (seg_starts + counts)[None, :])
                     & (row[:, None] < seg_ends[None, :]), axis=1)
    valid = jnp.logical_and(jnp.logical_not(in_gap), row < seg_ends[-1])
    _, sorted_tok = lax.sort_key_val(flat_e, jnp.arange(n_assign, dtype=jnp.int32) // TOP_K)
    row_tok = jnp.where(valid, sorted_tok[jnp.clip(row - shift_row, 0, n_assign - 1)], 0)
    xs = h[row_tok]
    xoff = (_lookup(seg_starts, block_e) + block_start
            - _lookup(pad_starts, block_e)).astype(jnp.int32)
    yb = _experts(block_e + layer * N_EXPERTS, xoff, n_real, xs, wi, bi, wo, bo)
    dest = (_lookup(pad_starts, flat_e) + flat_rank).reshape(n, TOP_K)
    yk = yb[dest.T.reshape(-1)]
    return _combine(x, gate_pad, yk)


def _block_diag(w):
    hh, d, _ = w.shape
    eye = jnp.eye(hh, dtype=w.dtype)
    return jnp.einsum('hij,hg->higj', w, eye).reshape(hh * d, hh * d)


def kernel(x_prompt, x_sample, cache_sb_k, cache_sb_v, page_table, state_lru_h, state_lru_conv, state_sconv, cache_mem_k, cache_mem_v, mem_prompt, ln_mix, w_in, gmlp_norm, gmlp_ws, gmlp_bs, lru_conv_w, lru_conv_b, lru_wa, lru_ba, lru_wx, lru_bx, lru_lambda, sb_q_norm, sb_k_norm, sb_bias, sconv_w, out_norm, w_out, ln_mem, ln_mem_kv, mem_wq, mem_wk, mem_wv, mem_q_norm, mem_k_norm, mem_wo, ln_moe, router_w, router_b, moe_w_in, moe_b_in, moe_w_out, moe_b_out):
    depth = w_in.shape[0]
    bp, sp, _ = x_prompt.shape
    bs, ss, _ = x_sample.shape
    n_p = bp * sp
    n_s = bs * ss
    n_pool = cache_sb_k.shape[1]

    x = jnp.concatenate([x_prompt.reshape(n_p, D_MODEL), x_sample.reshape(n_s, D_MODEL)], axis=0)
    mem2d = mem_prompt.reshape(bp * N_MEM, D_MODEL)
    pool_k = cache_sb_k.reshape(depth * n_pool, PAGE, GROUP_W)
    pool_v = cache_sb_v.reshape(depth * n_pool, PAGE, GROUP_W)
    cmem_k = cache_mem_k.reshape(depth * bs, N_MEM, MEM_W)
    cmem_v = cache_mem_v.reshape(depth * bs, N_MEM, MEM_W)

    w_in_b = w_in.astype(BF16)
    w_out_b = w_out.astype(BF16)
    mem_wq_b = mem_wq.astype(BF16)
    mem_wkv_b = jnp.concatenate([mem_wk, mem_wv], axis=2).astype(BF16)
    mem_wo_b = mem_wo.astype(BF16)
    moe_wi = moe_w_in.reshape(depth * N_EXPERTS, D_MODEL, 2 * D_EXPERT)
    moe_wo = moe_w_out.reshape(depth * N_EXPERTS, D_EXPERT, D_MODEL)
    moe_bi = moe_b_in.reshape(depth * N_EXPERTS, 1, 2 * D_EXPERT)
    moe_bo = moe_b_out.reshape(depth * N_EXPERTS, 1, D_MODEL)
    rw_b = jnp.pad(router_w, ((0, 0), (0, 0), (0, LANES - N_EXPERTS))).astype(BF16)
    rb_pad = jnp.pad(router_b, ((0, 0), (0, LANES - N_EXPERTS))).reshape(depth, 1, LANES)

    zeros_h = jnp.zeros((bp, 1, GROUP_W), F32)
    zeros_lb = jnp.zeros((bp, 3, GROUP_W), F32)
    zeros_sb = jnp.zeros((bp, 2, GROUP_W), F32)

    outs = [[] for _ in range(14)]
    for l in range(depth):
        mkv = _norm_matmul(mem2d, ln_mem_kv[l], mem_wkv_b[l],
                           head_gain=jnp.concatenate([jnp.tile(mem_k_norm[l], MEM_HEADS),
                                                      jnp.ones((MEM_W,), F32)]).reshape(1, 2 * MEM_W),
                           norm_cols=MEM_W, head_dim=MEM_HEAD_DIM)
        mk_p = mkv[:, :MEM_W].reshape(bp, N_MEM, MEM_W)
        mv_p = mkv[:, MEM_W:].reshape(bp, N_MEM, MEM_W)

        p_all = _norm_matmul(x, ln_mix[l], w_in_b[l])
        common = dict(
            gng=gmlp_norm[l].reshape(1, GROUP_W),
            cw=lru_conv_w[l], cb=lru_conv_b[l].reshape(1, GROUP_W),
            wa=_block_diag(lru_wa[l]).astype(BF16), ba=lru_ba[l].reshape(1, GROUP_W),
            wx=_block_diag(lru_wx[l]).astype(BF16), bx=lru_bx[l].reshape(1, GROUP_W),
            lam=lru_lambda[l].reshape(1, GROUP_W),
            qg=jnp.tile(sb_q_norm[l], N_HEADS).reshape(1, GROUP_W),
            kg=jnp.tile(sb_k_norm[l], N_HEADS).reshape(1, GROUP_W),
            sw=sconv_w[l])
        lp = min(CHUNK, sp)
        ls = min(CHUNK, ss)
        wts_p = dict(common, ws=gmlp_ws[l][:, :lp, :lp],
                     bsf=jnp.repeat(gmlp_bs[l][:, :lp].T, HEAD_DIM, axis=1))
        ws_exp = jnp.repeat(jnp.transpose(gmlp_ws[l][:, :ls, :ls], (2, 1, 0)), HEAD_DIM, axis=2)
        wts_s = dict(common, ws=ws_exp, bsf=jnp.repeat(gmlp_bs[l][:, :ls].T, HEAD_DIM, axis=1))
        (ya_p, yb_p, yd_p, q_p, k_p, hl_p, lb_p, sbuf_p, gv_p) = _mixer_pre(
            p_all, 0, bp, sp, lp, zeros_h, zeros_lb, zeros_sb, wts_p)
        (ya_s, yb_s, yd_s, q_s, k_s, hl_s, lb_s, sbuf_s, gv_s) = _mixer_pre(
            p_all, n_p, bs, ss, ls, state_lru_h[l].reshape(bs, 1, GROUP_W), state_lru_conv[l],
            state_sconv[l], wts_s)
        yc_p = _sb_prompt(sb_bias[l], q_p, k_p, p_all, bp, sp)
        yc_s = _sb_sample(page_table + l * n_pool, sb_bias[l], q_s, k_s, p_all, n_p,
                          pool_k, pool_v, bs, ss)
        cat = lambda a, b: jnp.concatenate([a, b], axis=0)
        x = _merge(cat(ya_p, ya_s), cat(yb_p, yb_s), cat(yc_p, yc_s), cat(yd_p, yd_s),
                   out_norm[l], w_out_b[l], x)

        q_mem = _norm_matmul(x, ln_mem[l], mem_wq_b[l],
                             head_gain=jnp.tile(mem_q_norm[l], MEM_HEADS).reshape(1, MEM_W),
                             norm_cols=MEM_W, head_dim=MEM_HEAD_DIM)
        a_p = _mem_attn(q_mem, 0, bp, sp, 512, mk_p, mv_p, 0)
        a_s = _mem_attn(q_mem, n_p, bs, ss, ss, cmem_k, cmem_v, l * bs)
        x = _matmul_res(cat(a_p, a_s), mem_wo_b[l], x)

        x = _moe(x, ln_moe[l].reshape(1, D_MODEL), rw_b[l], rb_pad[l],
                 moe_wi, moe_bi, moe_wo, moe_bo, l)

        v_all = p_all[:, 1536:1792]
        outs[0].append(k_p.reshape(bp, sp, N_HEADS, HEAD_DIM))
        outs[1].append(v_all[:n_p].reshape(bp, sp, N_HEADS, HEAD_DIM))
        outs[2].append(k_s.reshape(bs, ss, N_HEADS, HEAD_DIM))
        outs[3].append(v_all[n_p:].reshape(bs, ss, N_HEADS, HEAD_DIM))
        outs[4].append(hl_p.reshape(bp, GROUP_W))
        outs[5].append(hl_s.reshape(bs, GROUP_W))
        outs[6].append(lb_p)
        outs[7].append(lb_s)
        outs[8].append(sbuf_p)
        outs[9].append(sbuf_s)
        outs[10].append(gv_p.reshape(bp, lp, N_HEADS, HEAD_DIM))
        outs[11].append(gv_s.reshape(bs, ls, N_HEADS, HEAD_DIM))
        outs[12].append(mk_p.reshape(bp, N_MEM, MEM_HEADS, MEM_HEAD_DIM))
        outs[13].append(mv_p.reshape(bp, N_MEM, MEM_HEADS, MEM_HEAD_DIM))

    y_prompt = x[:n_p].reshape(bp, sp, D_MODEL)
    y_sample = x[n_p:].reshape(bs, ss, D_MODEL)
    return (y_prompt, y_sample) + tuple(jnp.stack(o) for o in outs)
```

```python
import functools
import math

import jax
import jax.numpy as jnp
from jax import lax
from jax.experimental import pallas as pl
from jax.experimental.pallas import tpu as pltpu

F32 = jnp.float32
BF16 = jnp.bfloat16

D_MODEL = 1024
GROUP_W = 256
N_HEADS = 4
HEAD_DIM = 64
HEAD_SHIFT = 6
PROJ_W = 10 * GROUP_W
CHUNK = 128
LRU_C = 8.0
N_MEM = 256
MEM_HEADS = 4
MEM_HEAD_DIM = 128
MEM_W = MEM_HEADS * MEM_HEAD_DIM
N_EXPERTS = 32
TOP_K = 4
D_EXPERT = 1024
SWIGLU_LIMIT = 7.0
SWIGLU_ALPHA = 1.702
RMS_EPS = 1e-6
PAGE = 128
LANES = 128

ROW_TILE = 512
MOE_ROWS = 512
MOE_HCHUNK = 512
VMEM_BIG = 48 * 1024 * 1024
VMEM_EXPERTS = 56 * 1024 * 1024


def _cparams(sem, vmem=None):
    return pltpu.CompilerParams(dimension_semantics=sem, vmem_limit_bytes=vmem)


def _bdot(a, b):
    return jnp.dot(a.astype(BF16), b.astype(BF16), preferred_element_type=F32)


def _gelu(x):
    return 0.5 * x * (1.0 + jnp.tanh(0.7978845608028654 * (x + 0.044715 * (x * x * x))))


def _rms(x, gain):
    return x * lax.rsqrt(jnp.mean(x * x, axis=-1, keepdims=True) + RMS_EPS) * gain


def _lane_head():
    return lax.broadcasted_iota(jnp.int32, (1, GROUP_W), 1) >> HEAD_SHIFT


def _split_dot(s, ones_bf16):
    hi = s.astype(BF16)
    lo = (s - hi.astype(F32)).astype(BF16)
    return jnp.dot(jnp.concatenate([hi, lo], axis=1), ones_bf16, preferred_element_type=F32)


def _norm_matmul_kernel(x_ref, g_ref, w_ref, hg_ref, o_ref, *, n_chunk, norm_cols, head_dim):
    h = _rms(x_ref[...], g_ref[...]).astype(BF16)
    n = o_ref.shape[1]
    for c0 in range(0, n, n_chunk):
        y = jnp.dot(h, w_ref[:, c0:c0 + n_chunk], preferred_element_type=F32)
        for h0 in range(0, n_chunk, head_dim):
            col = c0 + h0
            if col < norm_cols:
                yh = y[:, h0:h0 + head_dim]
                o_ref[:, col:col + head_dim] = _rms(yh, hg_ref[:, col:col + head_dim])
        if c0 + n_chunk > norm_cols:
            lo = max(c0, norm_cols)
            o_ref[:, lo:c0 + n_chunk] = y[:, lo - c0:]


def _norm_matmul(x, gain, w_bf16, head_gain=None, norm_cols=0, head_dim=LANES, n_chunk=512):
    m, k = x.shape
    n = w_bf16.shape[1]
    n_chunk = min(n_chunk, n)
    if head_gain is None:
        head_gain = jnp.ones((1, n), F32)
    kern = functools.partial(_norm_matmul_kernel, n_chunk=n_chunk, norm_cols=norm_cols,
                             head_dim=head_dim)
    return pl.pallas_call(
        kern,
        grid=(m // ROW_TILE,),
        in_specs=[pl.BlockSpec((ROW_TILE, k), lambda i: (i, 0)),
                  pl.BlockSpec((1, k), lambda i: (0, 0)),
                  pl.BlockSpec((k, n), lambda i: (0, 0)),
                  pl.BlockSpec((1, n), lambda i: (0, 0))],
        out_specs=pl.BlockSpec((ROW_TILE, n), lambda i: (i, 0)),
        out_shape=jax.ShapeDtypeStruct((m, n), F32),
        compiler_params=_cparams(("parallel",), VMEM_BIG),
        name="norm_matmul",
    )(x, gain.reshape(1, k), w_bf16, head_gain)


def _matmul_res_kernel(a_ref, w_ref, x_ref, o_ref):
    o_ref[...] = x_ref[...] + jnp.dot(a_ref[...].astype(BF16), w_ref[...],
                                      preferred_element_type=F32)


def _matmul_res(a, w_bf16, x):
    m, k = a.shape
    n = w_bf16.shape[1]
    return pl.pallas_call(
        _matmul_res_kernel,
        grid=(m // ROW_TILE,),
        in_specs=[pl.BlockSpec((ROW_TILE, k), lambda i: (i, 0)),
                  pl.BlockSpec((k, n), lambda i: (0, 0)),
                  pl.BlockSpec((ROW_TILE, n), lambda i: (i, 0))],
        out_specs=pl.BlockSpec((ROW_TILE, n), lambda i: (i, 0)),
        out_shape=jax.ShapeDtypeStruct((m, n), F32),
        compiler_params=_cparams(("parallel",), VMEM_BIG),
        name="matmul_res",
    )(a, w_bf16, x)


def _merge_kernel(ya_ref, yb_ref, yc_ref, yd_ref, g_ref, w_ref, x_ref, o_ref):
    parts = []
    for gi, y_ref in enumerate((ya_ref, yb_ref, yc_ref, yd_ref)):
        parts.append(_rms(y_ref[...], g_ref[gi:gi + 1, :]).astype(BF16))
    y = jnp.concatenate(parts, axis=1)
    o_ref[...] = x_ref[...] + jnp.dot(y, w_ref[...], preferred_element_type=F32)


def _merge(ya, yb, yc, yd, out_gain, w_bf16, x):
    m = x.shape[0]
    yspec = pl.BlockSpec((ROW_TILE, GROUP_W), lambda i: (i, 0))
    return pl.pallas_call(
        _merge_kernel,
        grid=(m // ROW_TILE,),
        in_specs=[yspec, yspec, yspec, yspec,
                  pl.BlockSpec((4, GROUP_W), lambda i: (0, 0)),
                  pl.BlockSpec((D_MODEL, D_MODEL), lambda i: (0, 0)),
                  pl.BlockSpec((ROW_TILE, D_MODEL), lambda i: (i, 0))],
        out_specs=pl.BlockSpec((ROW_TILE, D_MODEL), lambda i: (i, 0)),
        out_shape=jax.ShapeDtypeStruct((m, D_MODEL), F32),
        compiler_params=_cparams(("parallel",), VMEM_BIG),
        name="merge",
    )(ya, yb, yc, yd, out_gain, w_bf16, x)


def _shift_rows(x, s, fill):
    rolled = pltpu.roll(x, s, axis=0)
    row = lax.broadcasted_iota(jnp.int32, x.shape, 0)
    return jnp.where(row >= s, rolled, fill)


def _mixer_kernel(p_ref, h0_ref, lbuf_ref, sbuf_ref, gng_ref, ws_ref, bsf_ref, cw_ref, cb_ref,
                  wa_ref, ba_ref, wx_ref, bx_ref, lam_ref, qg_ref, kg_ref, sw_ref,
                  ya_ref, yb_ref, yd_ref, q_ref, k_ref, hl_ref, lbo_ref, sbo_ref, gv_ref,
                  xb_ref, eb_ref, h_ref, *, tb):
    t = pl.program_id(1)

    @pl.when(t == 0)
    def _():
        xb_ref[5:8, :] = lbuf_ref[0]
        eb_ref[6:8, :] = sbuf_ref[0]
        h_ref[...] = h0_ref[0]

    lane_head = _lane_head()

    a_u = p_ref[:, 0:256]
    a_v = p_ref[:, 256:512]
    v2 = _rms(_gelu(a_v), gng_ref[...])
    gv_ref[0] = v2
    mixed = bsf_ref[...]
    if tb == CHUNK:
        row = lax.broadcasted_iota(jnp.int32, (tb, tb), 0)
        col = lax.broadcasted_iota(jnp.int32, (tb, tb), 1)
        for hd in range(N_HEADS):
            w_h = jnp.where(row >= col, ws_ref[hd], 0.0)
            mixed = mixed + _bdot(w_h, jnp.where(lane_head == hd, v2, 0.0))
    else:
        row = lax.broadcasted_iota(jnp.int32, (tb, GROUP_W), 0)
        v2r = v2.astype(BF16).astype(F32)
        for s in range(tb):
            w_s = jnp.where(row >= s, ws_ref[s], 0.0).astype(BF16).astype(F32)
            mixed = mixed + w_s * v2r[s:s + 1, :]
    ya_ref[...] = _gelu(a_u) * mixed

    b_x = p_ref[:, 512:768]
    b_g = p_ref[:, 768:1024]
    xb_ref[8:8 + tb, :] = b_x
    conv = xb_ref[5:5 + tb, :] * cw_ref[0:1, :]
    for kk in range(1, 4):
        conv = conv + xb_ref[5 + kk:5 + kk + tb, :] * cw_ref[kk:kk + 1, :]
    tail = xb_ref[tb + 5:tb + 8, :]
    lbo_ref[0] = tail
    xb_ref[5:8, :] = tail
    xc = conv + cb_ref[...]
    xcb = xc.astype(BF16)
    r = jax.nn.sigmoid(jnp.dot(xcb, wa_ref[...], preferred_element_type=F32) + ba_ref[...])
    ig = jax.nn.sigmoid(jnp.dot(xcb, wx_ref[...], preferred_element_type=F32) + bx_ref[...])
    lam = lam_ref[...]
    log_sig_lam = -(jnp.maximum(-lam, 0.0) + jnp.log1p(jnp.exp(-jnp.abs(lam))))
    log_a = (LRU_C * r) * log_sig_lam
    a = jnp.exp(log_a)
    one_minus_a2 = -jnp.tanh(log_a) * (a * a + 1.0)
    bb = jnp.sqrt(one_minus_a2) * (ig * xc)
    aa = a
    s = 1
    while s < tb:
        a_sh = _shift_rows(aa, s, 1.0)
        b_sh = _shift_rows(bb, s, 0.0)
        bb = aa * b_sh + bb
        aa = aa * a_sh
        s *= 2
    hseq = aa * h_ref[...] + bb
    h_last = hseq[tb - 1:tb, :]
    h_ref[...] = h_last
    hl_ref[0] = h_last
    yb_ref[...] = hseq * _gelu(b_g)

    ones_bd = jnp.where(
        (lax.broadcasted_iota(jnp.int32, (GROUP_W, GROUP_W), 0) >> HEAD_SHIFT)
        == (lax.broadcasted_iota(jnp.int32, (GROUP_W, GROUP_W), 1) >> HEAD_SHIFT),
        1.0, 0.0).astype(BF16)
    for src, gain_ref, dst in ((1024, qg_ref, q_ref), (1280, kg_ref, k_ref)):
        xq = p_ref[:, src:src + 256]
        ms = _split_dot(xq * xq, jnp.concatenate([ones_bd, ones_bd], axis=0)) * (1.0 / HEAD_DIM)
        dst[...] = xq * lax.rsqrt(ms + RMS_EPS) * gain_ref[...]

    d_b = p_ref[:, 1792:2048]
    e = p_ref[:, 2048:2304] * p_ref[:, 2304:2560]
    eb_ref[8:8 + tb, :] = e
    dconv = eb_ref[6:6 + tb, :] * sw_ref[0:1, :]
    for kk in range(1, 3):
        dconv = dconv + eb_ref[6 + kk:6 + kk + tb, :] * sw_ref[kk:kk + 1, :]
    etail = eb_ref[tb + 6:tb + 8, :]
    sbo_ref[0] = etail
    eb_ref[6:8, :] = etail
    yd_ref[...] = d_b * dconv


def _mixer_pre(p_all, row0, batch, seq, tb, h0, lbuf, sbuf, wts):
    nt = seq // tb
    rb0 = row0 // tb
    rows = batch * seq
    rmap = lambda b, t: (b * nt + t, 0)
    cmap2 = lambda b, t: (0, 0)
    cmap3 = lambda b, t: (0, 0, 0)
    smap = lambda b, t: (b, 0, 0)
    yspec = pl.BlockSpec((tb, GROUP_W), rmap)
    vec = pl.BlockSpec((1, GROUP_W), cmap2)
    wsq = pl.BlockSpec((GROUP_W, GROUP_W), cmap2)
    ws = wts["ws"]
    y_sds = jax.ShapeDtypeStruct((rows, GROUP_W), F32)
    kern = functools.partial(_mixer_kernel, tb=tb)
    return pl.pallas_call(
        kern,
        grid=(batch, nt),
        in_specs=[pl.BlockSpec((tb, PROJ_W), lambda b, t: (rb0 + b * nt + t, 0)),
                  pl.BlockSpec((1, 1, GROUP_W), smap),
                  pl.BlockSpec((1, 3, GROUP_W), smap),
                  pl.BlockSpec((1, 2, GROUP_W), smap),
                  vec,
                  pl.BlockSpec(ws.shape, cmap3),
                  pl.BlockSpec((tb, GROUP_W), cmap2),
                  pl.BlockSpec((4, GROUP_W), cmap2), vec,
                  wsq, vec, wsq, vec, vec, vec, vec,
                  pl.BlockSpec((3, GROUP_W), cmap2)],
        out_specs=[yspec, yspec, yspec, yspec, yspec,
                   pl.BlockSpec((1, 1, GROUP_W), smap),
                   pl.BlockSpec((1, 3, GROUP_W), smap),
                   pl.BlockSpec((1, 2, GROUP_W), smap),
                   pl.BlockSpec((1, tb, GROUP_W), smap)],
        out_shape=[y_sds, y_sds, y_sds, y_sds, y_sds,
                   jax.ShapeDtypeStruct((batch, 1, GROUP_W), F32),
                   jax.ShapeDtypeStruct((batch, 3, GROUP_W), F32),
                   jax.ShapeDtypeStruct((batch, 2, GROUP_W), F32),
                   jax.ShapeDtypeStruct((batch, tb, GROUP_W), F32)],
        scratch_shapes=[pltpu.VMEM((tb + 8, GROUP_W), F32),
                        pltpu.VMEM((tb + 8, GROUP_W), F32),
                        pltpu.VMEM((1, GROUP_W), F32)],
        compiler_params=_cparams(("arbitrary", "arbitrary")),
        name="mixer_pre",
    )(p_all, h0, lbuf, sbuf, wts["gng"], ws, wts["bsf"], wts["cw"], wts["cb"],
      wts["wa"], wts["ba"], wts["wx"], wts["bx"], wts["lam"], wts["qg"], wts["kg"], wts["sw"])


def _sb_setup(q, bias_ref, rows_per_head):
    nr = N_HEADS * rows_per_head
    lane_head = _lane_head()
    qs = jnp.concatenate([jnp.where(lane_head == hd, q, 0.0) for hd in range(N_HEADS)],
                         axis=0).astype(BF16)
    row_head = lax.broadcasted_iota(jnp.int32, (nr, LANES), 0) >> int(math.log2(rows_per_head))
    bias = jnp.full((nr, LANES), bias_ref[N_HEADS - 1], F32)
    for hd in range(N_HEADS - 2, -1, -1):
        bias = jnp.where(row_head == hd, bias_ref[hd], bias)
    jj = lax.broadcasted_iota(jnp.int32, (LANES, 2 * LANES), 0)
    ss = lax.broadcasted_iota(jnp.int32, (LANES, 2 * LANES), 1)
    cum = jnp.where((jj > ss) | (ss >= LANES), 1.0, 0.0).astype(BF16)
    return qs, bias, jnp.concatenate([cum, cum], axis=0)


_NT = (((1,), (1,)), ((), ()))


def _sb_tile(qs, bias, cum, k, v, acc, c, mask, pages_transposed=False):
    kb = k.astype(BF16)
    vb = v.astype(BF16)
    if pages_transposed:
        z = jnp.dot(qs, kb, preferred_element_type=F32) + bias
    else:
        z = lax.dot_general(qs, kb, _NT, preferred_element_type=F32) + bias
    lk = -(jnp.maximum(z, 0.0) + jnp.log1p(jnp.exp(-jnp.abs(z))))
    lkm = lk if mask is None else jnp.where(mask, lk, 0.0)
    r = _split_dot(lkm, cum)
    w = jnp.exp(z + lk + r[:, :LANES] + c)
    if mask is not None:
        w = jnp.where(mask, w, 0.0)
    wb = w.astype(BF16)
    if pages_transposed:
        acc = acc + lax.dot_general(wb, vb, _NT, preferred_element_type=F32)
    else:
        acc = acc + jnp.dot(wb, vb, preferred_element_type=F32)
    return acc, c + r[:, LANES:]


def _sb_tile_ref(qs, bias, cum, k, v, acc_ref, c_ref, mask):
    acc, c = _sb_tile(qs, bias, cum, k, v, acc_ref[...], c_ref[...], mask)
    acc_ref[...] = acc
    c_ref[...] = c


def _sb_finish(acc, o_ref, rows_per_head):
    lane_head = _lane_head()
    out = jnp.zeros((rows_per_head, GROUP_W), F32)
    for hd in range(N_HEADS):
        out = out + jnp.where(lane_head == hd,
                              acc[hd * rows_per_head:(hd + 1) * rows_per_head, :], 0.0)
    o_ref[...] = out


def _sb_prompt_kernel(bias_ref, q_ref, k_ref, v_ref, o_ref, acc_ref, c_ref):
    i = pl.program_id(1)
    nr = N_HEADS * CHUNK
    qs, bias, cum = _sb_setup(q_ref[...] * (HEAD_DIM ** -0.5), bias_ref, CHUNK)
    acc_ref[...] = jnp.zeros_like(acc_ref)
    c_ref[...] = jnp.zeros_like(c_ref)
    tq = lax.broadcasted_iota(jnp.int32, (nr, LANES), 0) & (CHUNK - 1)
    ts = lax.broadcasted_iota(jnp.int32, (nr, LANES), 1)
    d0 = pl.multiple_of(i * CHUNK, CHUNK)
    _sb_tile_ref(qs, bias, cum, k_ref[pl.ds(d0, CHUNK), :], v_ref[pl.ds(d0, CHUNK), :],
                 acc_ref, c_ref, ts < tq)

    odd = i & 1

    @pl.when(odd == 1)
    def _():
        k0 = pl.multiple_of((i - 1) * CHUNK, CHUNK)
        _sb_tile_ref(qs, bias, cum, k_ref[pl.ds(k0, CHUNK), :], v_ref[pl.ds(k0, CHUNK), :],
                     acc_ref, c_ref, None)

    def body(j, carry):
        ka = pl.multiple_of((i - odd - 1 - 2 * j) * CHUNK, CHUNK)
        kb = pl.multiple_of((i - odd - 2 - 2 * j) * CHUNK, CHUNK)
        acc, c = _sb_tile(qs, bias, cum, k_ref[pl.ds(ka, CHUNK), :], v_ref[pl.ds(ka, CHUNK), :],
                          acc_ref[...], c_ref[...], None)
        acc, c = _sb_tile(qs, bias, cum, k_ref[pl.ds(kb, CHUNK), :], v_ref[pl.ds(kb, CHUNK), :],
                          acc, c, None)
        acc_ref[...] = acc
        c_ref[...] = c
        return carry

    lax.fori_loop(0, i >> 1, body, 0)
    _sb_finish(acc_ref[...], o_ref, CHUNK)


def _sb_prompt(sb_bias, q_p, k_p, p_all, batch, seq):
    nq = seq // CHUNK
    return pl.pallas_call(
        _sb_prompt_kernel,
        grid=(batch, nq),
        in_specs=[pl.BlockSpec(memory_space=pltpu.SMEM),
                  pl.BlockSpec((CHUNK, GROUP_W), lambda b, i: (b * nq + i, 0)),
                  pl.BlockSpec((seq, GROUP_W), lambda b, i: (b, 0)),
                  pl.BlockSpec((seq, GROUP_W), lambda b, i: (b, 6))],
        out_specs=pl.BlockSpec((CHUNK, GROUP_W), lambda b, i: (b * nq + i, 0)),
        out_shape=jax.ShapeDtypeStruct((batch * seq, GROUP_W), F32),
        scratch_shapes=[pltpu.VMEM((N_HEADS * CHUNK, GROUP_W), F32),
                        pltpu.VMEM((N_HEADS * CHUNK, LANES), F32)],
        compiler_params=_cparams(("parallel", "arbitrary")),
        name="sb_prompt",
    )(sb_bias, q_p, k_p, p_all)


def _sb_sample_kernel(pt_ref, bias_ref, q_ref, kn_ref, vn_ref, *rest, tq, n_pages):
    kp_refs = rest[:n_pages]
    vp_refs = rest[n_pages:2 * n_pages]
    o_ref, kpad_ref, vpad_ref = rest[2 * n_pages:]
    nr = N_HEADS * tq
    qs, bias, cum = _sb_setup(q_ref[...] * (HEAD_DIM ** -0.5), bias_ref, tq)
    kpad_ref[...] = jnp.zeros_like(kpad_ref)
    vpad_ref[...] = jnp.zeros_like(vpad_ref)
    kpad_ref[0:tq, :] = kn_ref[...]
    vpad_ref[0:tq, :] = vn_ref[...]
    trow = lax.broadcasted_iota(jnp.int32, (nr, LANES), 0) & (tq - 1)
    ts = lax.broadcasted_iota(jnp.int32, (nr, LANES), 1)
    acc = jnp.zeros((nr, GROUP_W), F32)
    c = jnp.zeros((nr, LANES), F32)
    acc, c = _sb_tile(qs, bias, cum, kpad_ref[...], vpad_ref[...], acc, c, ts < trow)
    for j in range(n_pages - 1, -1, -1):
        acc, c = _sb_tile(qs, bias, cum, kp_refs[j][0], vp_refs[j][0], acc, c, None,
                          pages_transposed=True)
    _sb_finish(acc, o_ref, tq)


def _sb_sample(page_rows, sb_bias, q_s, k_s, p_all, row0, pool_k, pool_v, batch, tq):
    n_pages = page_rows.shape[1]
    page_rows = page_rows.reshape(-1)
    rb0 = row0 // tq

    def page_spec(j):
        return pl.BlockSpec((1, GROUP_W, PAGE), lambda b, pt: (pt[b * n_pages + j], 0, 0))

    kern = functools.partial(_sb_sample_kernel, tq=tq, n_pages=n_pages)
    grid_spec = pltpu.PrefetchScalarGridSpec(
        num_scalar_prefetch=1,
        grid=(batch,),
        in_specs=([pl.BlockSpec(memory_space=pltpu.SMEM),
                   pl.BlockSpec((tq, GROUP_W), lambda b, pt: (b, 0)),
                   pl.BlockSpec((tq, GROUP_W), lambda b, pt: (b, 0)),
                   pl.BlockSpec((tq, GROUP_W), lambda b, pt: (rb0 + b, 6))]
                  + [page_spec(j) for j in range(n_pages)]
                  + [page_spec(j) for j in range(n_pages)]),
        out_specs=pl.BlockSpec((tq, GROUP_W), lambda b, pt: (b, 0)),
        scratch_shapes=[pltpu.VMEM((PAGE, GROUP_W), F32),
                        pltpu.VMEM((PAGE, GROUP_W), F32)])
    return pl.pallas_call(
        kern,
        grid_spec=grid_spec,
        out_shape=jax.ShapeDtypeStruct((batch * tq, GROUP_W), F32),
        compiler_params=_cparams(("parallel",)),
        name="sb_sample",
    )(page_rows, sb_bias, q_s, k_s, p_all, *([pool_k] * n_pages), *([pool_v] * n_pages))


def _mem_attn_kernel(q_ref, k_ref, v_ref, o_ref):
    scale = MEM_HEAD_DIM ** -0.5
    for hd in range(MEM_HEADS):
        sl = slice(hd * MEM_HEAD_DIM, (hd + 1) * MEM_HEAD_DIM)
        qh = q_ref[:, sl].astype(BF16)
        kh = k_ref[0, :, sl].astype(BF16)
        s = lax.dot_general(qh, kh, (((1,), (1,)), ((), ())), preferred_element_type=F32) * scale
        s = s - jnp.max(s, axis=-1, keepdims=True)
        e = jnp.exp(s)
        p = e / jnp.sum(e, axis=-1, keepdims=True)
        o_ref[:, sl] = jnp.dot(p.astype(BF16), v_ref[0, :, sl].astype(BF16),
                               preferred_element_type=F32)


def _mem_attn(q_all, row0, batch, seq, tq, mem_k, mem_v, kv0):
    nt = seq // tq
    rb0 = row0 // tq
    kvspec = pl.BlockSpec((1, N_MEM, MEM_W), lambda b, t: (kv0 + b, 0, 0))
    return pl.pallas_call(
        _mem_attn_kernel,
        grid=(batch, nt),
        in_specs=[pl.BlockSpec((tq, MEM_W), lambda b, t: (rb0 + b * nt + t, 0)), kvspec, kvspec],
        out_specs=pl.BlockSpec((tq, MEM_W), lambda b, t: (b * nt + t, 0)),
        out_shape=jax.ShapeDtypeStruct((batch * seq, MEM_W), F32),
        compiler_params=_cparams(("parallel", "arbitrary")),
        name="mem_attn",
    )(q_all, mem_k, mem_v)


def _router_kernel(x_ref, g_ref, w_ref, b_ref, h_ref, idx_ref, gate_ref, rank_ref, cnt_ref,
                   run_ref):
    i = pl.program_id(0)

    @pl.when(i == 0)
    def _():
        run_ref[...] = jnp.zeros_like(run_ref)

    hb = _rms(x_ref[...], g_ref[...]).astype(BF16)
    h_ref[...] = hb.astype(F32)
    logits = jnp.dot(hb, w_ref[...], preferred_element_type=F32) + b_ref[...]
    tm = logits.shape[0]
    lane = lax.broadcasted_iota(jnp.int32, logits.shape, 1).astype(F32)
    vals = jnp.where(lane < N_EXPERTS, logits, -jnp.inf)
    idx_out = jnp.zeros(logits.shape, F32)
    e_out = jnp.zeros(logits.shape, F32)
    picked = jnp.zeros(logits.shape, F32)
    sels = []
    m0 = None
    denom = None
    for kk in range(TOP_K):
        m = jnp.max(vals, axis=-1, keepdims=True)
        idx = jnp.min(jnp.where(vals == m, lane, float(LANES)), axis=-1, keepdims=True)
        sel = lane == idx
        sels.append(sel)
        picked = jnp.where(sel, 1.0, picked)
        vals = jnp.where(sel, -jnp.inf, vals)
        if kk == 0:
            m0 = m
        e = jnp.exp(m - m0)
        denom = e if kk == 0 else denom + e
        idx_out = jnp.where(lane == kk, idx, idx_out)
        e_out = jnp.where(lane == kk, e, e_out)
    idx_ref[...] = idx_out.astype(jnp.int32)
    gate_ref[...] = e_out / denom

    row = lax.broadcasted_iota(jnp.int32, (tm, tm), 0)
    col = lax.broadcasted_iota(jnp.int32, (tm, tm), 1)
    before = jnp.where(row > col, 1.0, 0.0).astype(BF16)
    prefix = jnp.dot(before, picked.astype(BF16), preferred_element_type=F32) + run_ref[...]
    rank_out = jnp.zeros(logits.shape, F32)
    for kk in range(TOP_K):
        rk = jnp.sum(jnp.where(sels[kk], prefix, 0.0), axis=-1, keepdims=True)
        rank_out = jnp.where(lane == kk, rk, rank_out)
    rank_ref[...] = rank_out.astype(jnp.int32)
    total = run_ref[...] + jnp.sum(picked, axis=0, keepdims=True)
    run_ref[...] = total
    cnt_ref[...] = total.astype(jnp.int32)


def _router(x, gain, w_bf16, bias):
    m = x.shape[0]
    rspec = pl.BlockSpec((ROW_TILE, LANES), lambda i: (i, 0))
    return pl.pallas_call(
        _router_kernel,
        grid=(m // ROW_TILE,),
        in_specs=[pl.BlockSpec((ROW_TILE, D_MODEL), lambda i: (i, 0)),
                  pl.BlockSpec((1, D_MODEL), lambda i: (0, 0)),
                  pl.BlockSpec((D_MODEL, LANES), lambda i: (0, 0)),
                  pl.BlockSpec((1, LANES), lambda i: (0, 0))],
        out_specs=[pl.BlockSpec((ROW_TILE, D_MODEL), lambda i: (i, 0)), rspec, rspec, rspec,
                   pl.BlockSpec((1, LANES), lambda i: (0, 0))],
        out_shape=[jax.ShapeDtypeStruct((m, D_MODEL), F32),
                   jax.ShapeDtypeStruct((m, LANES), jnp.int32),
                   jax.ShapeDtypeStruct((m, LANES), F32),
                   jax.ShapeDtypeStruct((m, LANES), jnp.int32),
                   jax.ShapeDtypeStruct((1, LANES), jnp.int32)],
        scratch_shapes=[pltpu.VMEM((1, LANES), F32)],
        compiler_params=_cparams(("arbitrary",)),
        name="router",
    )(x, gain, w_bf16, bias)


def _combine_kernel(x_ref, g_ref, y0_ref, y1_ref, y2_ref, y3_ref, o_ref):
    g = g_ref[...]
    y = ((y0_ref[...] * g[:, 0:1] + y1_ref[...] * g[:, 1:2])
         + (y2_ref[...] * g[:, 2:3] + y3_ref[...] * g[:, 3:4]))
    o_ref[...] = x_ref[...] + y


def _combine(x, gate_pad, yk):
    m, d = x.shape
    tm = 256
    nt = m // tm
    yspecs = [pl.BlockSpec((tm, d), lambda i, kk=kk: (kk * nt + i, 0)) for kk in range(TOP_K)]
    return pl.pallas_call(
        _combine_kernel,
        grid=(nt,),
        in_specs=[pl.BlockSpec((tm, d), lambda i: (i, 0)),
                  pl.BlockSpec((tm, LANES), lambda i: (i, 0))] + yspecs,
        out_specs=pl.BlockSpec((tm, d), lambda i: (i, 0)),
        out_shape=jax.ShapeDtypeStruct((m, d), F32),
        compiler_params=_cparams(("parallel",)),
        name="combine",
    )(x, gate_pad, yk, yk, yk, yk)


def _expert_kernel(be_ref, xoff_ref, nb_ref, xs_hbm, wi_ref, bi_ref, wo_ref, bo_ref, o_ref,
                   xbuf_ref, wib_ref, wob_ref, sem_ref):
    i = pl.program_id(0)
    nb = nb_ref[0]
    cur = i % 2

    def window_copy(step, buf):
        off = pl.multiple_of(xoff_ref[step], 8)
        return pltpu.make_async_copy(xs_hbm.at[pl.ds(off, MOE_ROWS), :], xbuf_ref.at[buf],
                                     sem_ref.at[buf])

    @pl.when(jnp.logical_and(i == 0, nb > 0))
    def _():
        window_copy(0, 0).start()

    @pl.when(i + 1 < nb)
    def _():
        window_copy(i + 1, 1 - cur).start()

    @pl.when(i < nb)
    def _():
        new_expert = jnp.logical_or(i == 0, be_ref[i] != be_ref[jnp.maximum(i - 1, 0)])

        @pl.when(new_expert)
        def _():
            rows = 64

            def cast_rows(r, carry):
                r0 = pl.multiple_of(r * rows, rows)
                wib_ref[pl.ds(r0, rows), :] = wi_ref[0, pl.ds(r0, rows), :].astype(BF16)
                wob_ref[pl.ds(r0, rows), :] = wo_ref[0, pl.ds(r0, rows), :].astype(BF16)
                return carry
            lax.fori_loop(0, D_MODEL // rows, cast_rows, 0)

        window_copy(i, cur).wait()
        x = xbuf_ref[cur].astype(BF16)
        acc = jnp.zeros(o_ref.shape, F32)
        for c0 in range(0, D_EXPERT, MOE_HCHUNK):
            g = jnp.dot(x, wib_ref[:, c0:c0 + MOE_HCHUNK], preferred_element_type=F32)
            g = g + bi_ref[0, :, c0:c0 + MOE_HCHUNK]
            u = jnp.dot(x, wib_ref[:, D_EXPERT + c0:D_EXPERT + c0 + MOE_HCHUNK],
                        preferred_element_type=F32)
            u = u + bi_ref[0, :, D_EXPERT + c0:D_EXPERT + c0 + MOE_HCHUNK]
            g = jnp.minimum(g, SWIGLU_LIMIT)
            u = jnp.clip(u, -SWIGLU_LIMIT, SWIGLU_LIMIT)
            act = g * jax.nn.sigmoid(SWIGLU_ALPHA * g) * (u + 1.0)
            acc = acc + jnp.dot(act.astype(BF16), wob_ref[c0:c0 + MOE_HCHUNK, :],
                                preferred_element_type=F32)
        o_ref[...] = acc + bo_ref[0]

    @pl.when(i >= nb)
    def _():
        o_ref[...] = jnp.zeros_like(o_ref)


def _experts(block_w, xoff, n_real, xs, wi, bi, wo, bo):
    n_blocks = block_w.shape[0]
    wmap = lambda i, bw, xo, nb: (bw[i], 0, 0)
    grid_spec = pltpu.PrefetchScalarGridSpec(
        num_scalar_prefetch=3,
        grid=(n_blocks,),
        in_specs=[pl.BlockSpec(memory_space=pl.ANY),
                  pl.BlockSpec((1, D_MODEL, 2 * D_EXPERT), wmap),
                  pl.BlockSpec((1, 1, 2 * D_EXPERT), wmap),
                  pl.BlockSpec((1, D_EXPERT, D_MODEL), wmap),
                  pl.BlockSpec((1, 1, D_MODEL), wmap)],
        out_specs=pl.BlockSpec((MOE_ROWS, D_MODEL), lambda i, bw, xo, nb: (i, 0)),
        scratch_shapes=[pltpu.VMEM((2, MOE_ROWS, D_MODEL), F32),
                        pltpu.VMEM((D_MODEL, 2 * D_EXPERT), BF16),
                        pltpu.VMEM((D_EXPERT, D_MODEL), BF16),
                        pltpu.SemaphoreType.DMA((2,))])
    return pl.pallas_call(
        _expert_kernel,
        grid_spec=grid_spec,
        out_shape=jax.ShapeDtypeStruct((n_blocks * MOE_ROWS, D_MODEL), F32),
        compiler_params=_cparams(("arbitrary",), VMEM_EXPERTS),
        name="experts",
    )(block_w, xoff, n_real, xs, wi, bi, wo, bo)


def _lookup(table, idx):
    experts = jnp.arange(N_EXPERTS, dtype=jnp.int32)
    return jnp.sum(jnp.where(idx[:, None] == experts[None, :], table[None, :], 0), axis=1)


def _moe(x, gain, w_router, rbias, wi, bi, wo, bo, layer):
    n = x.shape[0]
    h, idx_pad, gate_pad, rank_pad, cnt_pad = _router(x, gain, w_router, rbias)
    n_assign = n * TOP_K
    flat_e = idx_pad[:, :TOP_K].reshape(-1)
    flat_rank = rank_pad[:, :TOP_K].reshape(-1)
    counts = cnt_pad[0, :N_EXPERTS]
    starts = jnp.cumsum(counts) - counts
    padded = (counts + MOE_ROWS - 1) // MOE_ROWS * MOE_ROWS
    pad_ends = jnp.cumsum(padded)
    pad_starts = pad_ends - padded
    n_blocks = -(-n_assign // MOE_ROWS) + N_EXPERTS
    block_start = jnp.arange(n_blocks, dtype=jnp.int32) * MOE_ROWS
    block_e = jnp.minimum(jnp.sum(pad_ends[None, :] <= block_start[:, None], axis=1),
                          N_EXPERTS - 1).astype(jnp.int32)
    n_real = (pad_ends[-1] // MOE_ROWS).astype(jnp.int32).reshape(1)
    seg = (counts + 7) // 8 * 8
    seg_ends = jnp.cumsum(seg)
    seg_starts = seg_ends - seg
    n_in = n_assign + 8 * N_EXPERTS + MOE_ROWS
    row = jnp.arange(n_in, dtype=jnp.int32)
    shift = seg_starts - starts
    dshift = shift - jnp.concatenate([jnp.zeros((1,), jnp.int32), shift[:-1]])
    shift_row = jnp.sum(jnp.where(row[:, None] >= seg_starts[None, :], dshift[None, :], 0), axis=1)
    in_gap = jnp.any((row[:, None] >= (seg_starts + counts)[None, :])
                     & (row[:, None] < seg_ends[None, :]), axis=1)
    valid = jnp.logical_and(jnp.logical_not(in_gap), row < seg_ends[-1])
    _, sorted_tok = lax.sort_key_val(flat_e, jnp.arange(n_assign, dtype=jnp.int32) // TOP_K)
    row_tok = jnp.where(valid, sorted_tok[jnp.clip(row - shift_row, 0, n_assign - 1)], 0)
    xs = h[row_tok]
    xoff = (_lookup(seg_starts, block_e) + block_start
            - _lookup(pad_starts, block_e)).astype(jnp.int32)
    yb = _experts(block_e + layer * N_EXPERTS, xoff, n_real, xs, wi, bi, wo, bo)
    dest = (_lookup(pad_starts, flat_e) + flat_rank).reshape(n, TOP_K)
    yk = yb[dest.T.reshape(-1)]
    return _combine(x, gate_pad, yk)


def _block_diag(w):
    hh, d, _ = w.shape
    eye = jnp.eye(hh, dtype=w.dtype)
    return jnp.einsum('hij,hg->higj', w, eye).reshape(hh * d, hh * d)


def kernel(x_prompt, x_sample, cache_sb_k, cache_sb_v, page_table, state_lru_h, state_lru_conv, state_sconv, cache_mem_k, cache_mem_v, mem_prompt, ln_mix, w_in, gmlp_norm, gmlp_ws, gmlp_bs, lru_conv_w, lru_conv_b, lru_wa, lru_ba, lru_wx, lru_bx, lru_lambda, sb_q_norm, sb_k_norm, sb_bias, sconv_w, out_norm, w_out, ln_mem, ln_mem_kv, mem_wq, mem_wk, mem_wv, mem_q_norm, mem_k_norm, mem_wo, ln_moe, router_w, router_b, moe_w_in, moe_b_in, moe_w_out, moe_b_out):
    depth = w_in.shape[0]
    bp, sp, _ = x_prompt.shape
    bs, ss, _ = x_sample.shape
    n_p = bp * sp
    n_s = bs * ss
    n_pool = cache_sb_k.shape[1]

    x = jnp.concatenate([x_prompt.reshape(n_p, D_MODEL), x_sample.reshape(n_s, D_MODEL)], axis=0)
    mem2d = mem_prompt.reshape(bp * N_MEM, D_MODEL)
    pool_k = jnp.transpose(cache_sb_k, (0, 1, 3, 4, 2)).reshape(depth * n_pool, GROUP_W, PAGE)
    pool_v = jnp.transpose(cache_sb_v, (0, 1, 3, 4, 2)).reshape(depth * n_pool, GROUP_W, PAGE)
    cmem_k = cache_mem_k.reshape(depth * bs, N_MEM, MEM_W)
    cmem_v = cache_mem_v.reshape(depth * bs, N_MEM, MEM_W)

    w_in_b = w_in.astype(BF16)
    w_out_b = w_out.astype(BF16)
    mem_wq_b = mem_wq.astype(BF16)
    mem_wkv_b = jnp.concatenate([mem_wk, mem_wv], axis=2).astype(BF16)
    mem_wo_b = mem_wo.astype(BF16)
    moe_wi = moe_w_in.reshape(depth * N_EXPERTS, D_MODEL, 2 * D_EXPERT)
    moe_wo = moe_w_out.reshape(depth * N_EXPERTS, D_EXPERT, D_MODEL)
    moe_bi = moe_b_in.reshape(depth * N_EXPERTS, 1, 2 * D_EXPERT)
    moe_bo = moe_b_out.reshape(depth * N_EXPERTS, 1, D_MODEL)
    rw_b = jnp.pad(router_w, ((0, 0), (0, 0), (0, LANES - N_EXPERTS))).astype(BF16)
    rb_pad = jnp.pad(router_b, ((0, 0), (0, LANES - N_EXPERTS))).reshape(depth, 1, LANES)

    zeros_h = jnp.zeros((bp, 1, GROUP_W), F32)
    zeros_lb = jnp.zeros((bp, 3, GROUP_W), F32)
    zeros_sb = jnp.zeros((bp, 2, GROUP_W), F32)

    outs = [[] for _ in range(14)]
    for l in range(depth):
        mkv = _norm_matmul(mem2d, ln_mem_kv[l], mem_wkv_b[l],
                           head_gain=jnp.concatenate([jnp.tile(mem_k_norm[l], MEM_HEADS),
                                                      jnp.ones((MEM_W,), F32)]).reshape(1, 2 * MEM_W),
                           norm_cols=MEM_W, head_dim=MEM_HEAD_DIM)
        mk_p = mkv[:, :MEM_W].reshape(bp, N_MEM, MEM_W)
        mv_p = mkv[:, MEM_W:].reshape(bp, N_MEM, MEM_W)

        p_all = _norm_matmul(x, ln_mix[l], w_in_b[l])
        common = dict(
            gng=gmlp_norm[l].reshape(1, GROUP_W),
            cw=lru_conv_w[l], cb=lru_conv_b[l].reshape(1, GROUP_W),
            wa=_block_diag(lru_wa[l]).astype(BF16), ba=lru_ba[l].reshape(1, GROUP_W),
            wx=_block_diag(lru_wx[l]).astype(BF16), bx=lru_bx[l].reshape(1, GROUP_W),
            lam=lru_lambda[l].reshape(1, GROUP_W),
            qg=jnp.tile(sb_q_norm[l], N_HEADS).reshape(1, GROUP_W),
            kg=jnp.tile(sb_k_norm[l], N_HEADS).reshape(1, GROUP_W),
            sw=sconv_w[l])
        lp = min(CHUNK, sp)
        ls = min(CHUNK, ss)
        wts_p = dict(common, ws=gmlp_ws[l][:, :lp, :lp],
                     bsf=jnp.repeat(gmlp_bs[l][:, :lp].T, HEAD_DIM, axis=1))
        ws_exp = jnp.repeat(jnp.transpose(gmlp_ws[l][:, :ls, :ls], (2, 1, 0)), HEAD_DIM, axis=2)
        wts_s = dict(common, ws=ws_exp, bsf=jnp.repeat(gmlp_bs[l][:, :ls].T, HEAD_DIM, axis=1))
        (ya_p, yb_p, yd_p, q_p, k_p, hl_p, lb_p, sbuf_p, gv_p) = _mixer_pre(
            p_all, 0, bp, sp, lp, zeros_h, zeros_lb, zeros_sb, wts_p)
        (ya_s, yb_s, yd_s, q_s, k_s, hl_s, lb_s, sbuf_s, gv_s) = _mixer_pre(
            p_all, n_p, bs, ss, ls, state_lru_h[l].reshape(bs, 1, GROUP_W), state_lru_conv[l],
            state_sconv[l], wts_s)
        yc_p = _sb_prompt(sb_bias[l], q_p, k_p, p_all, bp, sp)
        yc_s = _sb_sample(page_table + l * n_pool, sb_bias[l], q_s, k_s, p_all, n_p,
                          pool_k, pool_v, bs, ss)
        cat = lambda a, b: jnp.concatenate([a, b], axis=0)
        x = _merge(cat(ya_p, ya_s), cat(yb_p, yb_s), cat(yc_p, yc_s), cat(yd_p, yd_s),
                   out_norm[l], w_out_b[l], x)

        q_mem = _norm_matmul(x, ln_mem[l], mem_wq_b[l],
                             head_gain=jnp.tile(mem_q_norm[l], MEM_HEADS).reshape(1, MEM_W),
                             norm_cols=MEM_W, head_dim=MEM_HEAD_DIM)
        a_p = _mem_attn(q_mem, 0, bp, sp, 512, mk_p, mv_p, 0)
        a_s = _mem_attn(q_mem, n_p, bs, ss, ss, cmem_k, cmem_v, l * bs)
        x = _matmul_res(cat(a_p, a_s), mem_wo_b[l], x)

        x = _moe(x, ln_moe[l].reshape(1, D_MODEL), rw_b[l], rb_pad[l],
                 moe_wi, moe_bi, moe_wo, moe_bo, l)

        v_all = p_all[:, 1536:1792]
        outs[0].append(k_p.reshape(bp, sp, N_HEADS, HEAD_DIM))
        outs[1].append(v_all[:n_p].reshape(bp, sp, N_HEADS, HEAD_DIM))
        outs[2].append(k_s.reshape(bs, ss, N_HEADS, HEAD_DIM))
        outs[3].append(v_all[n_p:].reshape(bs, ss, N_HEADS, HEAD_DIM))
        outs[4].append(hl_p.reshape(bp, GROUP_W))
        outs[5].append(hl_s.reshape(bs, GROUP_W))
        outs[6].append(lb_p)
        outs[7].append(lb_s)
        outs[8].append(sbuf_p)
        outs[9].append(sbuf_s)
        outs[10].append(gv_p.reshape(bp, lp, N_HEADS, HEAD_DIM))
        outs[11].append(gv_s.reshape(bs, ls, N_HEADS, HEAD_DIM))
        outs[12].append(mk_p.reshape(bp, N_MEM, MEM_HEADS, MEM_HEAD_DIM))
        outs[13].append(mv_p.reshape(bp, N_MEM, MEM_HEADS, MEM_HEAD_DIM))

    y_prompt = x[:n_p].reshape(bp, sp, D_MODEL)
    y_sample = x[n_p:].reshape(bs, ss, D_MODEL)
    return (y_prompt, y_sample) + tuple(jnp.stack(o) for o in outs)
```

```python
import functools
import math

import jax
import jax.numpy as jnp
from jax import lax
from jax.experimental import pallas as pl
from jax.experimental.pallas import tpu as pltpu

F32 = jnp.float32
BF16 = jnp.bfloat16

D_MODEL = 1024
GROUP_W = 256
N_HEADS = 4
HEAD_DIM = 64
HEAD_SHIFT = 6
PROJ_W = 10 * GROUP_W
CHUNK = 128
LRU_C = 8.0
N_MEM = 256
MEM_HEADS = 4
MEM_HEAD_DIM = 128
MEM_W = MEM_HEADS * MEM_HEAD_DIM
N_EXPERTS = 32
TOP_K = 4
D_EXPERT = 1024
SWIGLU_LIMIT = 7.0
SWIGLU_ALPHA = 1.702
RMS_EPS = 1e-6
PAGE = 128
LANES = 128

ROW_TILE = 512
MOE_ROWS = 512
MOE_HCHUNK = 512
VMEM_BIG = 48 * 1024 * 1024
VMEM_EXPERTS = 56 * 1024 * 1024


def _cparams(sem, vmem=None):
    return pltpu.CompilerParams(dimension_semantics=sem, vmem_limit_bytes=vmem)


def _bdot(a, b):
    return jnp.dot(a.astype(BF16), b.astype(BF16), preferred_element_type=F32)


def _gelu(x):
    return 0.5 * x * (1.0 + jnp.tanh(0.7978845608028654 * (x + 0.044715 * (x * x * x))))


def _rms(x, gain):
    return x * lax.rsqrt(jnp.mean(x * x, axis=-1, keepdims=True) + RMS_EPS) * gain


def _lane_head():
    return lax.broadcasted_iota(jnp.int32, (1, GROUP_W), 1) >> HEAD_SHIFT


def _split_dot(s, ones_bf16):
    hi = s.astype(BF16)
    lo = (s - hi.astype(F32)).astype(BF16)
    return jnp.dot(jnp.concatenate([hi, lo], axis=1), ones_bf16, preferred_element_type=F32)


def _norm_matmul_kernel(x_ref, g_ref, w_ref, hg_ref, o_ref, *, n_chunk, norm_cols, head_dim):
    h = _rms(x_ref[...], g_ref[...]).astype(BF16)
    n = o_ref.shape[1]
    for c0 in range(0, n, n_chunk):
        y = jnp.dot(h, w_ref[:, c0:c0 + n_chunk], preferred_element_type=F32)
        for h0 in range(0, n_chunk, head_dim):
            col = c0 + h0
            if col < norm_cols:
                yh = y[:, h0:h0 + head_dim]
                o_ref[:, col:col + head_dim] = _rms(yh, hg_ref[:, col:col + head_dim])
        if c0 + n_chunk > norm_cols:
            lo = max(c0, norm_cols)
            o_ref[:, lo:c0 + n_chunk] = y[:, lo - c0:]


def _norm_matmul(x, gain, w_bf16, head_gain=None, norm_cols=0, head_dim=LANES, n_chunk=512):
    m, k = x.shape
    n = w_bf16.shape[1]
    n_chunk = min(n_chunk, n)
    if head_gain is None:
        head_gain = jnp.ones((1, n), F32)
    kern = functools.partial(_norm_matmul_kernel, n_chunk=n_chunk, norm_cols=norm_cols,
                             head_dim=head_dim)
    return pl.pallas_call(
        kern,
        grid=(m // ROW_TILE,),
        in_specs=[pl.BlockSpec((ROW_TILE, k), lambda i: (i, 0)),
                  pl.BlockSpec((1, k), lambda i: (0, 0)),
                  pl.BlockSpec((k, n), lambda i: (0, 0)),
                  pl.BlockSpec((1, n), lambda i: (0, 0))],
        out_specs=pl.BlockSpec((ROW_TILE, n), lambda i: (i, 0)),
        out_shape=jax.ShapeDtypeStruct((m, n), F32),
        compiler_params=_cparams(("parallel",), VMEM_BIG),
        name="norm_matmul",
    )(x, gain.reshape(1, k), w_bf16, head_gain)


def _matmul_res_kernel(a_ref, w_ref, x_ref, o_ref):
    o_ref[...] = x_ref[...] + jnp.dot(a_ref[...].astype(BF16), w_ref[...],
                                      preferred_element_type=F32)


def _matmul_res(a, w_bf16, x):
    m, k = a.shape
    n = w_bf16.shape[1]
    return pl.pallas_call(
        _matmul_res_kernel,
        grid=(m // ROW_TILE,),
        in_specs=[pl.BlockSpec((ROW_TILE, k), lambda i: (i, 0)),
                  pl.BlockSpec((k, n), lambda i: (0, 0)),
                  pl.BlockSpec((ROW_TILE, n), lambda i: (i, 0))],
        out_specs=pl.BlockSpec((ROW_TILE, n), lambda i: (i, 0)),
        out_shape=jax.ShapeDtypeStruct((m, n), F32),
        compiler_params=_cparams(("parallel",), VMEM_BIG),
        name="matmul_res",
    )(a, w_bf16, x)


def _merge_kernel(ya_ref, yb_ref, yc_ref, yd_ref, g_ref, w_ref, x_ref, o_ref):
    parts = []
    for gi, y_ref in enumerate((ya_ref, yb_ref, yc_ref, yd_ref)):
        parts.append(_rms(y_ref[...], g_ref[gi:gi + 1, :]).astype(BF16))
    y = jnp.concatenate(parts, axis=1)
    o_ref[...] = x_ref[...] + jnp.dot(y, w_ref[...], preferred_element_type=F32)


def _merge(ya, yb, yc, yd, out_gain, w_bf16, x):
    m = x.shape[0]
    yspec = pl.BlockSpec((ROW_TILE, GROUP_W), lambda i: (i, 0))
    return pl.pallas_call(
        _merge_kernel,
        grid=(m // ROW_TILE,),
        in_specs=[yspec, yspec, yspec, yspec,
                  pl.BlockSpec((4, GROUP_W), lambda i: (0, 0)),
                  pl.BlockSpec((D_MODEL, D_MODEL), lambda i: (0, 0)),
                  pl.BlockSpec((ROW_TILE, D_MODEL), lambda i: (i, 0))],
        out_specs=pl.BlockSpec((ROW_TILE, D_MODEL), lambda i: (i, 0)),
        out_shape=jax.ShapeDtypeStruct((m, D_MODEL), F32),
        compiler_params=_cparams(("parallel",), VMEM_BIG),
        name="merge",
    )(ya, yb, yc, yd, out_gain, w_bf16, x)


def _shift_rows(x, s, fill):
    rolled = pltpu.roll(x, s, axis=0)
    row = lax.broadcasted_iota(jnp.int32, x.shape, 0)
    return jnp.where(row >= s, rolled, fill)


def _mixer_kernel(p_ref, h0_ref, lbuf_ref, sbuf_ref, gng_ref, ws_ref, bsf_ref, cw_ref, cb_ref,
                  wa_ref, ba_ref, wx_ref, bx_ref, lam_ref, qg_ref, kg_ref, sw_ref,
                  ya_ref, yb_ref, yd_ref, q_ref, k_ref, hl_ref, lbo_ref, sbo_ref, gv_ref,
                  xb_ref, eb_ref, h_ref, *, tb):
    t = pl.program_id(1)

    @pl.when(t == 0)
    def _():
        xb_ref[5:8, :] = lbuf_ref[0]
        eb_ref[6:8, :] = sbuf_ref[0]
        h_ref[...] = h0_ref[0]

    lane_head = _lane_head()

    a_u = p_ref[:, 0:256]
    a_v = p_ref[:, 256:512]
    v2 = _rms(_gelu(a_v), gng_ref[...])
    gv_ref[0] = v2
    mixed = bsf_ref[...]
    if tb == CHUNK:
        row = lax.broadcasted_iota(jnp.int32, (tb, tb), 0)
        col = lax.broadcasted_iota(jnp.int32, (tb, tb), 1)
        for hd in range(N_HEADS):
            w_h = jnp.where(row >= col, ws_ref[hd], 0.0)
            mixed = mixed + _bdot(w_h, jnp.where(lane_head == hd, v2, 0.0))
    else:
        row = lax.broadcasted_iota(jnp.int32, (tb, GROUP_W), 0)
        v2r = v2.astype(BF16).astype(F32)
        for s in range(tb):
            w_s = jnp.where(row >= s, ws_ref[s], 0.0).astype(BF16).astype(F32)
            mixed = mixed + w_s * v2r[s:s + 1, :]
    ya_ref[...] = _gelu(a_u) * mixed

    b_x = p_ref[:, 512:768]
    b_g = p_ref[:, 768:1024]
    xb_ref[8:8 + tb, :] = b_x
    conv = xb_ref[5:5 + tb, :] * cw_ref[0:1, :]
    for kk in range(1, 4):
        conv = conv + xb_ref[5 + kk:5 + kk + tb, :] * cw_ref[kk:kk + 1, :]
    tail = xb_ref[tb + 5:tb + 8, :]
    lbo_ref[0] = tail
    xb_ref[5:8, :] = tail
    xc = conv + cb_ref[...]
    xcb = xc.astype(BF16)
    r = jax.nn.sigmoid(jnp.dot(xcb, wa_ref[...], preferred_element_type=F32) + ba_ref[...])
    ig = jax.nn.sigmoid(jnp.dot(xcb, wx_ref[...], preferred_element_type=F32) + bx_ref[...])
    lam = lam_ref[...]
    log_sig_lam = -(jnp.maximum(-lam, 0.0) + jnp.log1p(jnp.exp(-jnp.abs(lam))))
    log_a = (LRU_C * r) * log_sig_lam
    a = jnp.exp(log_a)
    one_minus_a2 = -jnp.tanh(log_a) * (a * a + 1.0)
    bb = jnp.sqrt(one_minus_a2) * (ig * xc)
    aa = a
    s = 1
    while s < tb:
        a_sh = _shift_rows(aa, s, 1.0)
        b_sh = _shift_rows(bb, s, 0.0)
        bb = aa * b_sh + bb
        aa = aa * a_sh
        s *= 2
    hseq = aa * h_ref[...] + bb
    h_last = hseq[tb - 1:tb, :]
    h_ref[...] = h_last
    hl_ref[0] = h_last
    yb_ref[...] = hseq * _gelu(b_g)

    ones_bd = jnp.where(
        (lax.broadcasted_iota(jnp.int32, (GROUP_W, GROUP_W), 0) >> HEAD_SHIFT)
        == (lax.broadcasted_iota(jnp.int32, (GROUP_W, GROUP_W), 1) >> HEAD_SHIFT),
        1.0, 0.0).astype(BF16)
    for src, gain_ref, dst in ((1024, qg_ref, q_ref), (1280, kg_ref, k_ref)):
        xq = p_ref[:, src:src + 256]
        ms = _split_dot(xq * xq, jnp.concatenate([ones_bd, ones_bd], axis=0)) * (1.0 / HEAD_DIM)
        dst[...] = xq * lax.rsqrt(ms + RMS_EPS) * gain_ref[...]

    d_b = p_ref[:, 1792:2048]
    e = p_ref[:, 2048:2304] * p_ref[:, 2304:2560]
    eb_ref[8:8 + tb, :] = e
    dconv = eb_ref[6:6 + tb, :] * sw_ref[0:1, :]
    for kk in range(1, 3):
        dconv = dconv + eb_ref[6 + kk:6 + kk + tb, :] * sw_ref[kk:kk + 1, :]
    etail = eb_ref[tb + 6:tb + 8, :]
    sbo_ref[0] = etail
    eb_ref[6:8, :] = etail
    yd_ref[...] = d_b * dconv


def _mixer_pre(p_all, row0, batch, seq, tb, h0, lbuf, sbuf, wts):
    nt = seq // tb
    rb0 = row0 // tb
    rows = batch * seq
    rmap = lambda b, t: (b * nt + t, 0)
    cmap2 = lambda b, t: (0, 0)
    cmap3 = lambda b, t: (0, 0, 0)
    smap = lambda b, t: (b, 0, 0)
    yspec = pl.BlockSpec((tb, GROUP_W), rmap)
    vec = pl.BlockSpec((1, GROUP_W), cmap2)
    wsq = pl.BlockSpec((GROUP_W, GROUP_W), cmap2)
    ws = wts["ws"]
    y_sds = jax.ShapeDtypeStruct((rows, GROUP_W), F32)
    kern = functools.partial(_mixer_kernel, tb=tb)
    return pl.pallas_call(
        kern,
        grid=(batch, nt),
        in_specs=[pl.BlockSpec((tb, PROJ_W), lambda b, t: (rb0 + b * nt + t, 0)),
                  pl.BlockSpec((1, 1, GROUP_W), smap),
                  pl.BlockSpec((1, 3, GROUP_W), smap),
                  pl.BlockSpec((1, 2, GROUP_W), smap),
                  vec,
                  pl.BlockSpec(ws.shape, cmap3),
                  pl.BlockSpec((tb, GROUP_W), cmap2),
                  pl.BlockSpec((4, GROUP_W), cmap2), vec,
                  wsq, vec, wsq, vec, vec, vec, vec,
                  pl.BlockSpec((3, GROUP_W), cmap2)],
        out_specs=[yspec, yspec, yspec, yspec, yspec,
                   pl.BlockSpec((1, 1, GROUP_W), smap),
                   pl.BlockSpec((1, 3, GROUP_W), smap),
                   pl.BlockSpec((1, 2, GROUP_W), smap),
                   pl.BlockSpec((1, tb, GROUP_W), smap)],
        out_shape=[y_sds, y_sds, y_sds, y_sds, y_sds,
                   jax.ShapeDtypeStruct((batch, 1, GROUP_W), F32),
                   jax.ShapeDtypeStruct((batch, 3, GROUP_W), F32),
                   jax.ShapeDtypeStruct((batch, 2, GROUP_W), F32),
                   jax.ShapeDtypeStruct((batch, tb, GROUP_W), F32)],
        scratch_shapes=[pltpu.VMEM((tb + 8, GROUP_W), F32),
                        pltpu.VMEM((tb + 8, GROUP_W), F32),
                        pltpu.VMEM((1, GROUP_W), F32)],
        compiler_params=_cparams(("arbitrary", "arbitrary")),
        name="mixer_pre",
    )(p_all, h0, lbuf, sbuf, wts["gng"], ws, wts["bsf"], wts["cw"], wts["cb"],
      wts["wa"], wts["ba"], wts["wx"], wts["bx"], wts["lam"], wts["qg"], wts["kg"], wts["sw"])


def _sb_setup(q, bias_ref, rows_per_head):
    nr = N_HEADS * rows_per_head
    lane_head = _lane_head()
    qs = jnp.concatenate([jnp.where(lane_head == hd, q, 0.0) for hd in range(N_HEADS)],
                         axis=0).astype(BF16)
    row_head = lax.broadcasted_iota(jnp.int32, (nr, LANES), 0) >> int(math.log2(rows_per_head))
    bias = jnp.full((nr, LANES), bias_ref[N_HEADS - 1], F32)
    for hd in range(N_HEADS - 2, -1, -1):
        bias = jnp.where(row_head == hd, bias_ref[hd], bias)
    jj = lax.broadcasted_iota(jnp.int32, (LANES, 2 * LANES), 0)
    ss = lax.broadcasted_iota(jnp.int32, (LANES, 2 * LANES), 1)
    cum = jnp.where((jj > ss) | (ss >= LANES), 1.0, 0.0).astype(BF16)
    return qs, bias, jnp.concatenate([cum, cum], axis=0)


_NT = (((1,), (1,)), ((), ()))


def _sb_tile(qs, bias, cum, k, v, acc, c, mask):
    z = lax.dot_general(qs, k.astype(BF16), _NT, preferred_element_type=F32) + bias
    lk = -(jnp.maximum(z, 0.0) + jnp.log1p(jnp.exp(-jnp.abs(z))))
    lkm = lk if mask is None else jnp.where(mask, lk, 0.0)
    r = _split_dot(lkm, cum)
    w = jnp.exp(z + lk + r[:, :LANES] + c)
    if mask is not None:
        w = jnp.where(mask, w, 0.0)
    acc = acc + jnp.dot(w.astype(BF16), v.astype(BF16), preferred_element_type=F32)
    return acc, c + r[:, LANES:]


def _sb_tile_ref(qs, bias, cum, k, v, acc_ref, c_ref, mask):
    acc, c = _sb_tile(qs, bias, cum, k, v, acc_ref[...], c_ref[...], mask)
    acc_ref[...] = acc
    c_ref[...] = c


def _sb_finish(acc, o_ref, rows_per_head):
    lane_head = _lane_head()
    out = jnp.zeros((rows_per_head, GROUP_W), F32)
    for hd in range(N_HEADS):
        out = out + jnp.where(lane_head == hd,
                              acc[hd * rows_per_head:(hd + 1) * rows_per_head, :], 0.0)
    o_ref[...] = out


def _sb_prompt_kernel(bias_ref, q_ref, k_ref, v_ref, o_ref, acc_ref, c_ref):
    i = pl.program_id(1)
    nr = N_HEADS * CHUNK
    qs, bias, cum = _sb_setup(q_ref[...] * (HEAD_DIM ** -0.5), bias_ref, CHUNK)
    acc_ref[...] = jnp.zeros_like(acc_ref)
    c_ref[...] = jnp.zeros_like(c_ref)
    tq = lax.broadcasted_iota(jnp.int32, (nr, LANES), 0) & (CHUNK - 1)
    ts = lax.broadcasted_iota(jnp.int32, (nr, LANES), 1)
    d0 = pl.multiple_of(i * CHUNK, CHUNK)
    _sb_tile_ref(qs, bias, cum, k_ref[pl.ds(d0, CHUNK), :], v_ref[pl.ds(d0, CHUNK), :],
                 acc_ref, c_ref, ts < tq)

    odd = i & 1

    @pl.when(odd == 1)
    def _():
        k0 = pl.multiple_of((i - 1) * CHUNK, CHUNK)
        _sb_tile_ref(qs, bias, cum, k_ref[pl.ds(k0, CHUNK), :], v_ref[pl.ds(k0, CHUNK), :],
                     acc_ref, c_ref, None)

    def body(j, carry):
        ka = pl.multiple_of((i - odd - 1 - 2 * j) * CHUNK, CHUNK)
        kb = pl.multiple_of((i - odd - 2 - 2 * j) * CHUNK, CHUNK)
        acc, c = _sb_tile(qs, bias, cum, k_ref[pl.ds(ka, CHUNK), :], v_ref[pl.ds(ka, CHUNK), :],
                          acc_ref[...], c_ref[...], None)
        acc, c = _sb_tile(qs, bias, cum, k_ref[pl.ds(kb, CHUNK), :], v_ref[pl.ds(kb, CHUNK), :],
                          acc, c, None)
        acc_ref[...] = acc
        c_ref[...] = c
        return carry

    lax.fori_loop(0, i >> 1, body, 0)
    _sb_finish(acc_ref[...], o_ref, CHUNK)


def _sb_prompt(sb_bias, q_p, k_p, p_all, batch, seq):
    nq = seq // CHUNK
    return pl.pallas_call(
        _sb_prompt_kernel,
        grid=(batch, nq),
        in_specs=[pl.BlockSpec(memory_space=pltpu.SMEM),
                  pl.BlockSpec((CHUNK, GROUP_W), lambda b, i: (b * nq + i, 0)),
                  pl.BlockSpec((seq, GROUP_W), lambda b, i: (b, 0)),
                  pl.BlockSpec((seq, GROUP_W), lambda b, i: (b, 6))],
        out_specs=pl.BlockSpec((CHUNK, GROUP_W), lambda b, i: (b * nq + i, 0)),
        out_shape=jax.ShapeDtypeStruct((batch * seq, GROUP_W), F32),
        scratch_shapes=[pltpu.VMEM((N_HEADS * CHUNK, GROUP_W), F32),
                        pltpu.VMEM((N_HEADS * CHUNK, LANES), F32)],
        compiler_params=_cparams(("parallel", "arbitrary")),
        name="sb_prompt",
    )(sb_bias, q_p, k_p, p_all)


def _sb_sample_kernel(pt_ref, bias_ref, q_ref, kn_ref, vn_ref, *rest, tq, n_pages):
    kp_refs = rest[:n_pages]
    vp_refs = rest[n_pages:2 * n_pages]
    o_ref, kpad_ref, vpad_ref = rest[2 * n_pages:]
    nr = N_HEADS * tq
    qs, bias, cum = _sb_setup(q_ref[...] * (HEAD_DIM ** -0.5), bias_ref, tq)
    kpad_ref[...] = jnp.zeros_like(kpad_ref)
    vpad_ref[...] = jnp.zeros_like(vpad_ref)
    kpad_ref[0:tq, :] = kn_ref[...]
    vpad_ref[0:tq, :] = vn_ref[...]
    trow = lax.broadcasted_iota(jnp.int32, (nr, LANES), 0) & (tq - 1)
    ts = lax.broadcasted_iota(jnp.int32, (nr, LANES), 1)
    mask = ts < trow
    zs = [lax.dot_general(qs, kpad_ref[...].astype(BF16), _NT, preferred_element_type=F32) + bias]
    for j in range(n_pages - 1, -1, -1):
        zs.append(jnp.dot(qs, kp_refs[j][0].astype(BF16), preferred_element_type=F32) + bias)
    lks = [-(jnp.maximum(z, 0.0) + jnp.log1p(jnp.exp(-jnp.abs(z)))) for z in zs]
    lkms = [jnp.where(mask, lks[0], 0.0)] + lks[1:]
    r_all = _split_dot(jnp.concatenate(lkms, axis=0), cum)
    acc = jnp.zeros((nr, GROUP_W), F32)
    c = jnp.zeros((nr, LANES), F32)
    for b in range(n_pages + 1):
        r = r_all[b * nr:(b + 1) * nr, :]
        w = jnp.exp(zs[b] + lks[b] + r[:, :LANES] + c)
        if b == 0:
            w = jnp.where(mask, w, 0.0)
            acc = acc + jnp.dot(w.astype(BF16), vpad_ref[...].astype(BF16),
                                preferred_element_type=F32)
        else:
            acc = acc + lax.dot_general(w.astype(BF16), vp_refs[n_pages - b][0].astype(BF16), _NT,
                                        preferred_element_type=F32)
        c = c + r[:, LANES:]
    _sb_finish(acc, o_ref, tq)


def _sb_sample(page_rows, sb_bias, q_s, k_s, p_all, row0, pool_k, pool_v, batch, tq):
    n_pages = page_rows.shape[1]
    page_rows = page_rows.reshape(-1)
    rb0 = row0 // tq

    def page_spec(j):
        return pl.BlockSpec((1, GROUP_W, PAGE), lambda b, pt: (pt[b * n_pages + j], 0, 0))

    kern = functools.partial(_sb_sample_kernel, tq=tq, n_pages=n_pages)
    grid_spec = pltpu.PrefetchScalarGridSpec(
        num_scalar_prefetch=1,
        grid=(batch,),
        in_specs=([pl.BlockSpec(memory_space=pltpu.SMEM),
                   pl.BlockSpec((tq, GROUP_W), lambda b, pt: (b, 0)),
                   pl.BlockSpec((tq, GROUP_W), lambda b, pt: (b, 0)),
                   pl.BlockSpec((tq, GROUP_W), lambda b, pt: (rb0 + b, 6))]
                  + [page_spec(j) for j in range(n_pages)]
                  + [page_spec(j) for j in range(n_pages)]),
        out_specs=pl.BlockSpec((tq, GROUP_W), lambda b, pt: (b, 0)),
        scratch_shapes=[pltpu.VMEM((PAGE, GROUP_W), F32),
                        pltpu.VMEM((PAGE, GROUP_W), F32)])
    return pl.pallas_call(
        kern,
        grid_spec=grid_spec,
        out_shape=jax.ShapeDtypeStruct((batch * tq, GROUP_W), F32),
        compiler_params=_cparams(("parallel",)),
        name="sb_sample",
    )(page_rows, sb_bias, q_s, k_s, p_all, *([pool_k] * n_pages), *([pool_v] * n_pages))


def _mem_attn_kernel(q_ref, k_ref, v_ref, o_ref):
    scale = MEM_HEAD_DIM ** -0.5
    for hd in range(MEM_HEADS):
        sl = slice(hd * MEM_HEAD_DIM, (hd + 1) * MEM_HEAD_DIM)
        qh = q_ref[:, sl].astype(BF16)
        kh = k_ref[0, :, sl].astype(BF16)
        s = lax.dot_general(qh, kh, (((1,), (1,)), ((), ())), preferred_element_type=F32) * scale
        s = s - jnp.max(s, axis=-1, keepdims=True)
        e = jnp.exp(s)
        p = e / jnp.sum(e, axis=-1, keepdims=True)
        o_ref[:, sl] = jnp.dot(p.astype(BF16), v_ref[0, :, sl].astype(BF16),
                               preferred_element_type=F32)


def _mem_attn(q_all, row0, batch, seq, tq, mem_k, mem_v, kv0):
    nt = seq // tq
    rb0 = row0 // tq
    kvspec = pl.BlockSpec((1, N_MEM, MEM_W), lambda b, t: (kv0 + b, 0, 0))
    return pl.pallas_call(
        _mem_attn_kernel,
        grid=(batch, nt),
        in_specs=[pl.BlockSpec((tq, MEM_W), lambda b, t: (rb0 + b * nt + t, 0)), kvspec, kvspec],
        out_specs=pl.BlockSpec((tq, MEM_W), lambda b, t: (b * nt + t, 0)),
        out_shape=jax.ShapeDtypeStruct((batch * seq, MEM_W), F32),
        compiler_params=_cparams(("parallel", "arbitrary")),
        name="mem_attn",
    )(q_all, mem_k, mem_v)


def _router_kernel(x_ref, g_ref, w_ref, b_ref, h_ref, idx_ref, gate_ref, rank_ref, cnt_ref,
                   run_ref):
    i = pl.program_id(0)

    @pl.when(i == 0)
    def _():
        run_ref[...] = jnp.zeros_like(run_ref)

    hb = _rms(x_ref[...], g_ref[...]).astype(BF16)
    h_ref[...] = hb.astype(F32)
    logits = jnp.dot(hb, w_ref[...], preferred_element_type=F32) + b_ref[...]
    tm = logits.shape[0]
    lane = lax.broadcasted_iota(jnp.int32, logits.shape, 1).astype(F32)
    vals = jnp.where(lane < N_EXPERTS, logits, -jnp.inf)
    idx_out = jnp.zeros(logits.shape, F32)
    e_out = jnp.zeros(logits.shape, F32)
    picked = jnp.zeros(logits.shape, F32)
    sels = []
    m0 = None
    denom = None
    for kk in range(TOP_K):
        m = jnp.max(vals, axis=-1, keepdims=True)
        idx = jnp.min(jnp.where(vals == m, lane, float(LANES)), axis=-1, keepdims=True)
        sel = lane == idx
        sels.append(sel)
        picked = jnp.where(sel, 1.0, picked)
        vals = jnp.where(sel, -jnp.inf, vals)
        if kk == 0:
            m0 = m
        e = jnp.exp(m - m0)
        denom = e if kk == 0 else denom + e
        idx_out = jnp.where(lane == kk, idx, idx_out)
        e_out = jnp.where(lane == kk, e, e_out)
    idx_ref[...] = idx_out.astype(jnp.int32)
    gate_ref[...] = e_out / denom

    row = lax.broadcasted_iota(jnp.int32, (tm, tm), 0)
    col = lax.broadcasted_iota(jnp.int32, (tm, tm), 1)
    before = jnp.where(row > col, 1.0, 0.0).astype(BF16)
    prefix = jnp.dot(before, picked.astype(BF16), preferred_element_type=F32) + run_ref[...]
    rank_out = jnp.zeros(logits.shape, F32)
    for kk in range(TOP_K):
        rk = jnp.sum(jnp.where(sels[kk], prefix, 0.0), axis=-1, keepdims=True)
        rank_out = jnp.where(lane == kk, rk, rank_out)
    rank_ref[...] = rank_out.astype(jnp.int32)
    total = run_ref[...] + jnp.sum(picked, axis=0, keepdims=True)
    run_ref[...] = total
    cnt_ref[...] = total.astype(jnp.int32)


def _router(x, gain, w_bf16, bias):
    m = x.shape[0]
    rspec = pl.BlockSpec((ROW_TILE, LANES), lambda i: (i, 0))
    return pl.pallas_call(
        _router_kernel,
        grid=(m // ROW_TILE,),
        in_specs=[pl.BlockSpec((ROW_TILE, D_MODEL), lambda i: (i, 0)),
                  pl.BlockSpec((1, D_MODEL), lambda i: (0, 0)),
                  pl.BlockSpec((D_MODEL, LANES), lambda i: (0, 0)),
                  pl.BlockSpec((1, LANES), lambda i: (0, 0))],
        out_specs=[pl.BlockSpec((ROW_TILE, D_MODEL), lambda i: (i, 0)), rspec, rspec, rspec,
                   pl.BlockSpec((1, LANES), lambda i: (0, 0))],
        out_shape=[jax.ShapeDtypeStruct((m, D_MODEL), F32),
                   jax.ShapeDtypeStruct((m, LANES), jnp.int32),
                   jax.ShapeDtypeStruct((m, LANES), F32),
                   jax.ShapeDtypeStruct((m, LANES), jnp.int32),
                   jax.ShapeDtypeStruct((1, LANES), jnp.int32)],
        scratch_shapes=[pltpu.VMEM((1, LANES), F32)],
        compiler_params=_cparams(("arbitrary",)),
        name="router",
    )(x, gain, w_bf16, bias)


def _combine_kernel(x_ref, g_ref, y0_ref, y1_ref, y2_ref, y3_ref, o_ref):
    g = g_ref[...]
    y = ((y0_ref[...] * g[:, 0:1] + y1_ref[...] * g[:, 1:2])
         + (y2_ref[...] * g[:, 2:3] + y3_ref[...] * g[:, 3:4]))
    o_ref[...] = x_ref[...] + y


def _combine(x, gate_pad, yk):
    m, d = x.shape
    tm = 256
    nt = m // tm
    yspecs = [pl.BlockSpec((tm, d), lambda i, kk=kk: (kk * nt + i, 0)) for kk in range(TOP_K)]
    return pl.pallas_call(
        _combine_kernel,
        grid=(nt,),
        in_specs=[pl.BlockSpec((tm, d), lambda i: (i, 0)),
                  pl.BlockSpec((tm, LANES), lambda i: (i, 0))] + yspecs,
        out_specs=pl.BlockSpec((tm, d), lambda i: (i, 0)),
        out_shape=jax.ShapeDtypeStruct((m, d), F32),
        compiler_params=_cparams(("parallel",)),
        name="combine",
    )(x, gate_pad, yk, yk, yk, yk)


def _expert_kernel(be_ref, xoff_ref, nb_ref, xs_hbm, wi_ref, bi_ref, wo_ref, bo_ref, o_ref,
                   xbuf_ref, wib_ref, wob_ref, sem_ref):
    i = pl.program_id(0)
    nb = nb_ref[0]
    cur = i % 2

    def window_copy(step, buf):
        off = pl.multiple_of(xoff_ref[step], 8)
        return pltpu.make_async_copy(xs_hbm.at[pl.ds(off, MOE_ROWS), :], xbuf_ref.at[buf],
                                     sem_ref.at[buf])

    @pl.when(jnp.logical_and(i == 0, nb > 0))
    def _():
        window_copy(0, 0).start()

    @pl.when(i + 1 < nb)
    def _():
        window_copy(i + 1, 1 - cur).start()

    @pl.when(i < nb)
    def _():
        new_expert = jnp.logical_or(i == 0, be_ref[i] != be_ref[jnp.maximum(i - 1, 0)])

        @pl.when(new_expert)
        def _():
            rows = 64

            def cast_rows(r, carry):
                r0 = pl.multiple_of(r * rows, rows)
                wib_ref[pl.ds(r0, rows), :] = wi_ref[0, pl.ds(r0, rows), :].astype(BF16)
                wob_ref[pl.ds(r0, rows), :] = wo_ref[0, pl.ds(r0, rows), :].astype(BF16)
                return carry
            lax.fori_loop(0, D_MODEL // rows, cast_rows, 0)

        window_copy(i, cur).wait()
        x = xbuf_ref[cur].astype(BF16)
        acc = jnp.zeros(o_ref.shape, F32)
        for c0 in range(0, D_EXPERT, MOE_HCHUNK):
            g = jnp.dot(x, wib_ref[:, c0:c0 + MOE_HCHUNK], preferred_element_type=F32)
            g = g + bi_ref[0, :, c0:c0 + MOE_HCHUNK]
            u = jnp.dot(x, wib_ref[:, D_EXPERT + c0:D_EXPERT + c0 + MOE_HCHUNK],
                        preferred_element_type=F32)
            u = u + bi_ref[0, :, D_EXPERT + c0:D_EXPERT + c0 + MOE_HCHUNK]
            g = jnp.minimum(g, SWIGLU_LIMIT)
            u = jnp.clip(u, -SWIGLU_LIMIT, SWIGLU_LIMIT)
            act = g * jax.nn.sigmoid(SWIGLU_ALPHA * g) * (u + 1.0)
            acc = acc + jnp.dot(act.astype(BF16), wob_ref[c0:c0 + MOE_HCHUNK, :],
                                preferred_element_type=F32)
        o_ref[...] = acc + bo_ref[0]

    @pl.when(i >= nb)
    def _():
        o_ref[...] = jnp.zeros_like(o_ref)


def _experts(block_w, xoff, n_real, xs, wi, bi, wo, bo):
    n_blocks = block_w.shape[0]
    wmap = lambda i, bw, xo, nb: (bw[i], 0, 0)
    grid_spec = pltpu.PrefetchScalarGridSpec(
        num_scalar_prefetch=3,
        grid=(n_blocks,),
        in_specs=[pl.BlockSpec(memory_space=pl.ANY),
                  pl.BlockSpec((1, D_MODEL, 2 * D_EXPERT), wmap),
                  pl.BlockSpec((1, 1, 2 * D_EXPERT), wmap),
                  pl.BlockSpec((1, D_EXPERT, D_MODEL), wmap),
                  pl.BlockSpec((1, 1, D_MODEL), wmap)],
        out_specs=pl.BlockSpec((MOE_ROWS, D_MODEL), lambda i, bw, xo, nb: (i, 0)),
        scratch_shapes=[pltpu.VMEM((2, MOE_ROWS, D_MODEL), F32),
                        pltpu.VMEM((D_MODEL, 2 * D_EXPERT), BF16),
                        pltpu.VMEM((D_EXPERT, D_MODEL), BF16),
                        pltpu.SemaphoreType.DMA((2,))])
    return pl.pallas_call(
        _expert_kernel,
        grid_spec=grid_spec,
        out_shape=jax.ShapeDtypeStruct((n_blocks * MOE_ROWS, D_MODEL), F32),
        compiler_params=_cparams(("arbitrary",), VMEM_EXPERTS),
        name="experts",
    )(block_w, xoff, n_real, xs, wi, bi, wo, bo)


def _lookup(table, idx):
    experts = jnp.arange(N_EXPERTS, dtype=jnp.int32)
    return jnp.sum(jnp.where(idx[:, None] == experts[None, :], table[None, :], 0), axis=1)


def _moe(x, gain, w_router, rbias, wi, bi, wo, bo, layer):
    n = x.shape[0]
    h, idx_pad, gate_pad, rank_pad, cnt_pad = _router(x, gain, w_router, rbias)
    n_assign = n * TOP_K
    flat_e = idx_pad[:, :TOP_K].reshape(-1)
    flat_rank = rank_pad[:, :TOP_K].reshape(-1)
    counts = cnt_pad[0, :N_EXPERTS]
    starts = jnp.cumsum(counts) - counts
    padded = (counts + MOE_ROWS - 1) // MOE_ROWS * MOE_ROWS
    pad_ends = jnp.cumsum(padded)
    pad_starts = pad_ends - padded
    n_blocks = -(-n_assign // MOE_ROWS) + N_EXPERTS
    block_start = jnp.arange(n_blocks, dtype=jnp.int32) * MOE_ROWS
    block_e = jnp.minimum(jnp.sum(pad_ends[None, :] <= block_start[:, None], axis=1),
                          N_EXPERTS - 1).astype(jnp.int32)
    n_real = (pad_ends[-1] // MOE_ROWS).astype(jnp.int32).reshape(1)
    seg = (counts + 7) // 8 * 8
    seg_ends = jnp.cumsum(seg)
    seg_starts = seg_ends - seg
    n_in = n_assign + 8 * N_EXPERTS + MOE_ROWS
    row = jnp.arange(n_in, dtype=jnp.int32)
    shift = seg_starts - starts
    dshift = shift - jnp.concatenate([jnp.zeros((1,), jnp.int32), shift[:-1]])
    shift_row = jnp.sum(jnp.where(row[:, None] >= seg_starts[None, :], dshift[None, :], 0), axis=1)
    in_gap = jnp.any((row[:, None] >= (seg_starts + counts)[None, :])
                     & (row[:, None] < seg_ends[None, :]), axis=1)
    valid = jnp.logical_and(jnp.logical_not(in_gap), row < seg_ends[-1])
    tok_bits = max(n - 1, 1).bit_length()
    packed = (flat_e << tok_bits) | (jnp.arange(n_assign, dtype=jnp.int32) // TOP_K)
    sorted_tok = lax.sort(packed) & ((1 << tok_bits) - 1)
    row_tok = jnp.where(valid, sorted_tok[jnp.clip(row - shift_row, 0, n_assign - 1)], 0)
    xs = h[row_tok]
    xoff = (_lookup(seg_starts, block_e) + block_start
            - _lookup(pad_starts, block_e)).astype(jnp.int32)
    yb = _experts(block_e + layer * N_EXPERTS, xoff, n_real, xs, wi, bi, wo, bo)
    dest = (_lookup(pad_starts, flat_e) + flat_rank).reshape(n, TOP_K)
    yk = yb[dest.T.reshape(-1)]
    return _combine(x, gate_pad, yk)


def _block_diag(w):
    hh, d, _ = w.shape
    eye = jnp.eye(hh, dtype=w.dtype)
    return jnp.einsum('hij,hg->higj', w, eye).reshape(hh * d, hh * d)


def kernel(x_prompt, x_sample, cache_sb_k, cache_sb_v, page_table, state_lru_h, state_lru_conv, state_sconv, cache_mem_k, cache_mem_v, mem_prompt, ln_mix, w_in, gmlp_norm, gmlp_ws, gmlp_bs, lru_conv_w, lru_conv_b, lru_wa, lru_ba, lru_wx, lru_bx, lru_lambda, sb_q_norm, sb_k_norm, sb_bias, sconv_w, out_norm, w_out, ln_mem, ln_mem_kv, mem_wq, mem_wk, mem_wv, mem_q_norm, mem_k_norm, mem_wo, ln_moe, router_w, router_b, moe_w_in, moe_b_in, moe_w_out, moe_b_out):
    depth = w_in.shape[0]
    bp, sp, _ = x_prompt.shape
    bs, ss, _ = x_sample.shape
    n_p = bp * sp
    n_s = bs * ss
    n_pool = cache_sb_k.shape[1]

    x = jnp.concatenate([x_prompt.reshape(n_p, D_MODEL), x_sample.reshape(n_s, D_MODEL)], axis=0)
    mem2d = mem_prompt.reshape(bp * N_MEM, D_MODEL)
    pool_k = jnp.transpose(cache_sb_k, (0, 1, 3, 4, 2)).reshape(depth * n_pool, GROUP_W, PAGE)
    pool_v = jnp.transpose(cache_sb_v, (0, 1, 3, 4, 2)).reshape(depth * n_pool, GROUP_W, PAGE)
    cmem_k = cache_mem_k.reshape(depth * bs, N_MEM, MEM_W)
    cmem_v = cache_mem_v.reshape(depth * bs, N_MEM, MEM_W)

    w_in_b = w_in.astype(BF16)
    w_out_b = w_out.astype(BF16)
    mem_wq_b = mem_wq.astype(BF16)
    mem_wkv_b = jnp.concatenate([mem_wk, mem_wv], axis=2).astype(BF16)
    mem_wo_b = mem_wo.astype(BF16)
    moe_wi = moe_w_in.reshape(depth * N_EXPERTS, D_MODEL, 2 * D_EXPERT)
    moe_wo = moe_w_out.reshape(depth * N_EXPERTS, D_EXPERT, D_MODEL)
    moe_bi = moe_b_in.reshape(depth * N_EXPERTS, 1, 2 * D_EXPERT)
    moe_bo = moe_b_out.reshape(depth * N_EXPERTS, 1, D_MODEL)
    rw_b = jnp.pad(router_w, ((0, 0), (0, 0), (0, LANES - N_EXPERTS))).astype(BF16)
    rb_pad = jnp.pad(router_b, ((0, 0), (0, LANES - N_EXPERTS))).reshape(depth, 1, LANES)

    zeros_h = jnp.zeros((bp, 1, GROUP_W), F32)
    zeros_lb = jnp.zeros((bp, 3, GROUP_W), F32)
    zeros_sb = jnp.zeros((bp, 2, GROUP_W), F32)

    outs = [[] for _ in range(14)]
    for l in range(depth):
        mkv = _norm_matmul(mem2d, ln_mem_kv[l], mem_wkv_b[l],
                           head_gain=jnp.concatenate([jnp.tile(mem_k_norm[l], MEM_HEADS),
                                                      jnp.ones((MEM_W,), F32)]).reshape(1, 2 * MEM_W),
                           norm_cols=MEM_W, head_dim=MEM_HEAD_DIM)
        mk_p = mkv[:, :MEM_W].reshape(bp, N_MEM, MEM_W)
        mv_p = mkv[:, MEM_W:].reshape(bp, N_MEM, MEM_W)

        p_all = _norm_matmul(x, ln_mix[l], w_in_b[l])
        common = dict(
            gng=gmlp_norm[l].reshape(1, GROUP_W),
            cw=lru_conv_w[l], cb=lru_conv_b[l].reshape(1, GROUP_W),
            wa=_block_diag(lru_wa[l]).astype(BF16), ba=lru_ba[l].reshape(1, GROUP_W),
            wx=_block_diag(lru_wx[l]).astype(BF16), bx=lru_bx[l].reshape(1, GROUP_W),
            lam=lru_lambda[l].reshape(1, GROUP_W),
            qg=jnp.tile(sb_q_norm[l], N_HEADS).reshape(1, GROUP_W),
            kg=jnp.tile(sb_k_norm[l], N_HEADS).reshape(1, GROUP_W),
            sw=sconv_w[l])
        lp = min(CHUNK, sp)
        ls = min(CHUNK, ss)
        wts_p = dict(common, ws=gmlp_ws[l][:, :lp, :lp],
                     bsf=jnp.repeat(gmlp_bs[l][:, :lp].T, HEAD_DIM, axis=1))
        ws_exp = jnp.repeat(jnp.transpose(gmlp_ws[l][:, :ls, :ls], (2, 1, 0)), HEAD_DIM, axis=2)
        wts_s = dict(common, ws=ws_exp, bsf=jnp.repeat(gmlp_bs[l][:, :ls].T, HEAD_DIM, axis=1))
        (ya_p, yb_p, yd_p, q_p, k_p, hl_p, lb_p, sbuf_p, gv_p) = _mixer_pre(
            p_all, 0, bp, sp, lp, zeros_h, zeros_lb, zeros_sb, wts_p)
        (ya_s, yb_s, yd_s, q_s, k_s, hl_s, lb_s, sbuf_s, gv_s) = _mixer_pre(
            p_all, n_p, bs, ss, ls, state_lru_h[l].reshape(bs, 1, GROUP_W), state_lru_conv[l],
            state_sconv[l], wts_s)
        yc_p = _sb_prompt(sb_bias[l], q_p, k_p, p_all, bp, sp)
        yc_s = _sb_sample(page_table + l * n_pool, sb_bias[l], q_s, k_s, p_all, n_p,
                          pool_k, pool_v, bs, ss)
        cat = lambda a, b: jnp.concatenate([a, b], axis=0)
        x = _merge(cat(ya_p, ya_s), cat(yb_p, yb_s), cat(yc_p, yc_s), cat(yd_p, yd_s),
                   out_norm[l], w_out_b[l], x)

        q_mem = _norm_matmul(x, ln_mem[l], mem_wq_b[l],
                             head_gain=jnp.tile(mem_q_norm[l], MEM_HEADS).reshape(1, MEM_W),
                             norm_cols=MEM_W, head_dim=MEM_HEAD_DIM)
        a_p = _mem_attn(q_mem, 0, bp, sp, 512, mk_p, mv_p, 0)
        a_s = _mem_attn(q_mem, n_p, bs, ss, ss, cmem_k, cmem_v, l * bs)
        x = _matmul_res(cat(a_p, a_s), mem_wo_b[l], x)

        x = _moe(x, ln_moe[l].reshape(1, D_MODEL), rw_b[l], rb_pad[l],
                 moe_wi, moe_bi, moe_wo, moe_bo, l)

        v_all = p_all[:, 1536:1792]
        outs[0].append(k_p.reshape(bp, sp, N_HEADS, HEAD_DIM))
        outs[1].append(v_all[:n_p].reshape(bp, sp, N_HEADS, HEAD_DIM))
        outs[2].append(k_s.reshape(bs, ss, N_HEADS, HEAD_DIM))
        outs[3].append(v_all[n_p:].reshape(bs, ss, N_HEADS, HEAD_DIM))
        outs[4].append(hl_p.reshape(bp, GROUP_W))
        outs[5].append(hl_s.reshape(bs, GROUP_W))
        outs[6].append(lb_p)
        outs[7].append(lb_s)
        outs[8].append(sbuf_p)
        outs[9].append(sbuf_s)
        outs[10].append(gv_p.reshape(bp, lp, N_HEADS, HEAD_DIM))
        outs[11].append(gv_s.reshape(bs, ls, N_HEADS, HEAD_DIM))
        outs[12].append(mk_p.reshape(bp, N_MEM, MEM_HEADS, MEM_HEAD_DIM))
        outs[13].append(mv_p.reshape(bp, N_MEM, MEM_HEADS, MEM_HEAD_DIM))

    y_prompt = x[:n_p].reshape(bp, sp, D_MODEL)
    y_sample = x[n_p:].reshape(bs, ss, D_MODEL)
    return (y_prompt, y_sample) + tuple(jnp.stack(o) for o in outs)
```

```python
import functools
import math

import jax
import jax.numpy as jnp
from jax import lax
from jax.experimental import pallas as pl
from jax.experimental.pallas import tpu as pltpu

F32 = jnp.float32
BF16 = jnp.bfloat16

D_MODEL = 1024
GROUP_W = 256
N_HEADS = 4
HEAD_DIM = 64
HEAD_SHIFT = 6
PROJ_W = 10 * GROUP_W
CHUNK = 128
LRU_C = 8.0
N_MEM = 256
MEM_HEADS = 4
MEM_HEAD_DIM = 128
MEM_W = MEM_HEADS * MEM_HEAD_DIM
N_EXPERTS = 32
TOP_K = 4
D_EXPERT = 1024
SWIGLU_LIMIT = 7.0
SWIGLU_ALPHA = 1.702
RMS_EPS = 1e-6
PAGE = 128
LANES = 128

ROW_TILE = 512
MOE_ROWS = 512
MOE_HCHUNK = 512
VMEM_BIG = 48 * 1024 * 1024
VMEM_EXPERTS = 56 * 1024 * 1024


def _cparams(sem, vmem=None):
    return pltpu.CompilerParams(dimension_semantics=sem, vmem_limit_bytes=vmem)


def _bdot(a, b):
    return jnp.dot(a.astype(BF16), b.astype(BF16), preferred_element_type=F32)


def _gelu(x):
    return 0.5 * x * (1.0 + jnp.tanh(0.7978845608028654 * (x + 0.044715 * (x * x * x))))


def _rms(x, gain):
    return x * lax.rsqrt(jnp.mean(x * x, axis=-1, keepdims=True) + RMS_EPS) * gain


def _lane_head():
    return lax.broadcasted_iota(jnp.int32, (1, GROUP_W), 1) >> HEAD_SHIFT


def _split_dot(s, ones_bf16):
    hi = s.astype(BF16)
    lo = (s - hi.astype(F32)).astype(BF16)
    return jnp.dot(jnp.concatenate([hi, lo], axis=1), ones_bf16, preferred_element_type=F32)


def _norm_matmul_kernel(x_ref, g_ref, w_ref, hg_ref, o_ref, *, n_chunk, norm_cols, head_dim):
    h = _rms(x_ref[...], g_ref[...]).astype(BF16)
    n = o_ref.shape[1]
    for c0 in range(0, n, n_chunk):
        y = jnp.dot(h, w_ref[:, c0:c0 + n_chunk], preferred_element_type=F32)
        for h0 in range(0, n_chunk, head_dim):
            col = c0 + h0
            if col < norm_cols:
                yh = y[:, h0:h0 + head_dim]
                o_ref[:, col:col + head_dim] = _rms(yh, hg_ref[:, col:col + head_dim])
        if c0 + n_chunk > norm_cols:
            lo = max(c0, norm_cols)
            o_ref[:, lo:c0 + n_chunk] = y[:, lo - c0:]


def _norm_matmul(x, gain, w_bf16, head_gain=None, norm_cols=0, head_dim=LANES, n_chunk=512):
    m, k = x.shape
    n = w_bf16.shape[1]
    n_chunk = min(n_chunk, n)
    if head_gain is None:
        head_gain = jnp.ones((1, n), F32)
    kern = functools.partial(_norm_matmul_kernel, n_chunk=n_chunk, norm_cols=norm_cols,
                             head_dim=head_dim)
    return pl.pallas_call(
        kern,
        grid=(m // ROW_TILE,),
        in_specs=[pl.BlockSpec((ROW_TILE, k), lambda i: (i, 0)),
                  pl.BlockSpec((1, k), lambda i: (0, 0)),
                  pl.BlockSpec((k, n), lambda i: (0, 0)),
                  pl.BlockSpec((1, n), lambda i: (0, 0))],
        out_specs=pl.BlockSpec((ROW_TILE, n), lambda i: (i, 0)),
        out_shape=jax.ShapeDtypeStruct((m, n), F32),
        compiler_params=_cparams(("parallel",), VMEM_BIG),
        name="norm_matmul",
    )(x, gain.reshape(1, k), w_bf16, head_gain)


def _matmul_res_kernel(a_ref, w_ref, x_ref, o_ref):
    o_ref[...] = x_ref[...] + jnp.dot(a_ref[...].astype(BF16), w_ref[...],
                                      preferred_element_type=F32)


def _matmul_res(a, w_bf16, x):
    m, k = a.shape
    n = w_bf16.shape[1]
    return pl.pallas_call(
        _matmul_res_kernel,
        grid=(m // ROW_TILE,),
        in_specs=[pl.BlockSpec((ROW_TILE, k), lambda i: (i, 0)),
                  pl.BlockSpec((k, n), lambda i: (0, 0)),
                  pl.BlockSpec((ROW_TILE, n), lambda i: (i, 0))],
        out_specs=pl.BlockSpec((ROW_TILE, n), lambda i: (i, 0)),
        out_shape=jax.ShapeDtypeStruct((m, n), F32),
        compiler_params=_cparams(("parallel",), VMEM_BIG),
        name="matmul_res",
    )(a, w_bf16, x)


def _merge_kernel(ya_ref, yb_ref, yc_ref, yd_ref, g_ref, w_ref, x_ref, o_ref):
    parts = []
    for gi, y_ref in enumerate((ya_ref, yb_ref, yc_ref, yd_ref)):
        parts.append(_rms(y_ref[...], g_ref[gi:gi + 1, :]).astype(BF16))
    y = jnp.concatenate(parts, axis=1)
    o_ref[...] = x_ref[...] + jnp.dot(y, w_ref[...], preferred_element_type=F32)


def _merge(ya, yb, yc, yd, out_gain, w_bf16, x):
    m = x.shape[0]
    yspec = pl.BlockSpec((ROW_TILE, GROUP_W), lambda i: (i, 0))
    return pl.pallas_call(
        _merge_kernel,
        grid=(m // ROW_TILE,),
        in_specs=[yspec, yspec, yspec, yspec,
                  pl.BlockSpec((4, GROUP_W), lambda i: (0, 0)),
                  pl.BlockSpec((D_MODEL, D_MODEL), lambda i: (0, 0)),
                  pl.BlockSpec((ROW_TILE, D_MODEL), lambda i: (i, 0))],
        out_specs=pl.BlockSpec((ROW_TILE, D_MODEL), lambda i: (i, 0)),
        out_shape=jax.ShapeDtypeStruct((m, D_MODEL), F32),
        compiler_params=_cparams(("parallel",), VMEM_BIG),
        name="merge",
    )(ya, yb, yc, yd, out_gain, w_bf16, x)


def _shift_rows(x, s, fill):
    rolled = pltpu.roll(x, s, axis=0)
    row = lax.broadcasted_iota(jnp.int32, x.shape, 0)
    return jnp.where(row >= s, rolled, fill)


def _mixer_kernel(p_ref, h0_ref, lbuf_ref, sbuf_ref, gng_ref, ws_ref, bsf_ref, cw_ref, cb_ref,
                  wa_ref, ba_ref, wx_ref, bx_ref, lam_ref, qg_ref, kg_ref, sw_ref,
                  ya_ref, yb_ref, yd_ref, q_ref, k_ref, hl_ref, lbo_ref, sbo_ref, gv_ref,
                  xb_ref, eb_ref, h_ref, *, tb):
    t = pl.program_id(1)

    @pl.when(t == 0)
    def _():
        xb_ref[5:8, :] = lbuf_ref[0]
        eb_ref[6:8, :] = sbuf_ref[0]
        h_ref[...] = h0_ref[0]

    lane_head = _lane_head()

    a_u = p_ref[:, 0:256]
    a_v = p_ref[:, 256:512]
    v2 = _rms(_gelu(a_v), gng_ref[...])
    gv_ref[0] = v2
    mixed = bsf_ref[...]
    if tb == CHUNK:
        row = lax.broadcasted_iota(jnp.int32, (tb, tb), 0)
        col = lax.broadcasted_iota(jnp.int32, (tb, tb), 1)
        for hd in range(N_HEADS):
            w_h = jnp.where(row >= col, ws_ref[hd], 0.0)
            mixed = mixed + _bdot(w_h, jnp.where(lane_head == hd, v2, 0.0))
    else:
        row = lax.broadcasted_iota(jnp.int32, (tb, GROUP_W), 0)
        v2r = v2.astype(BF16).astype(F32)
        for s in range(tb):
            w_s = jnp.where(row >= s, ws_ref[s], 0.0).astype(BF16).astype(F32)
            mixed = mixed + w_s * v2r[s:s + 1, :]
    ya_ref[...] = _gelu(a_u) * mixed

    b_x = p_ref[:, 512:768]
    b_g = p_ref[:, 768:1024]
    xb_ref[8:8 + tb, :] = b_x
    conv = xb_ref[5:5 + tb, :] * cw_ref[0:1, :]
    for kk in range(1, 4):
        conv = conv + xb_ref[5 + kk:5 + kk + tb, :] * cw_ref[kk:kk + 1, :]
    tail = xb_ref[tb + 5:tb + 8, :]
    lbo_ref[0] = tail
    xb_ref[5:8, :] = tail
    xc = conv + cb_ref[...]
    xcb = xc.astype(BF16)
    r = jax.nn.sigmoid(jnp.dot(xcb, wa_ref[...], preferred_element_type=F32) + ba_ref[...])
    ig = jax.nn.sigmoid(jnp.dot(xcb, wx_ref[...], preferred_element_type=F32) + bx_ref[...])
    lam = lam_ref[...]
    log_sig_lam = -(jnp.maximum(-lam, 0.0) + jnp.log1p(jnp.exp(-jnp.abs(lam))))
    log_a = (LRU_C * r) * log_sig_lam
    a = jnp.exp(log_a)
    one_minus_a2 = -jnp.tanh(log_a) * (a * a + 1.0)
    bb = jnp.sqrt(one_minus_a2) * (ig * xc)
    aa = a
    s = 1
    while s < tb:
        a_sh = _shift_rows(aa, s, 1.0)
        b_sh = _shift_rows(bb, s, 0.0)
        bb = aa * b_sh + bb
        aa = aa * a_sh
        s *= 2
    hseq = aa * h_ref[...] + bb
    h_last = hseq[tb - 1:tb, :]
    h_ref[...] = h_last
    hl_ref[0] = h_last
    yb_ref[...] = hseq * _gelu(b_g)

    ones_bd = jnp.where(
        (lax.broadcasted_iota(jnp.int32, (GROUP_W, GROUP_W), 0) >> HEAD_SHIFT)
        == (lax.broadcasted_iota(jnp.int32, (GROUP_W, GROUP_W), 1) >> HEAD_SHIFT),
        1.0, 0.0).astype(BF16)
    for src, gain_ref, dst in ((1024, qg_ref, q_ref), (1280, kg_ref, k_ref)):
        xq = p_ref[:, src:src + 256]
        ms = _split_dot(xq * xq, jnp.concatenate([ones_bd, ones_bd], axis=0)) * (1.0 / HEAD_DIM)
        dst[...] = xq * lax.rsqrt(ms + RMS_EPS) * gain_ref[...]

    d_b = p_ref[:, 1792:2048]
    e = p_ref[:, 2048:2304] * p_ref[:, 2304:2560]
    eb_ref[8:8 + tb, :] = e
    dconv = eb_ref[6:6 + tb, :] * sw_ref[0:1, :]
    for kk in range(1, 3):
        dconv = dconv + eb_ref[6 + kk:6 + kk + tb, :] * sw_ref[kk:kk + 1, :]
    etail = eb_ref[tb + 6:tb + 8, :]
    sbo_ref[0] = etail
    eb_ref[6:8, :] = etail
    yd_ref[...] = d_b * dconv


def _mixer_pre(p_all, row0, batch, seq, tb, h0, lbuf, sbuf, wts):
    nt = seq // tb
    rb0 = row0 // tb
    rows = batch * seq
    rmap = lambda b, t: (b * nt + t, 0)
    cmap2 = lambda b, t: (0, 0)
    cmap3 = lambda b, t: (0, 0, 0)
    smap = lambda b, t: (b, 0, 0)
    yspec = pl.BlockSpec((tb, GROUP_W), rmap)
    vec = pl.BlockSpec((1, GROUP_W), cmap2)
    wsq = pl.BlockSpec((GROUP_W, GROUP_W), cmap2)
    ws = wts["ws"]
    y_sds = jax.ShapeDtypeStruct((rows, GROUP_W), F32)
    kern = functools.partial(_mixer_kernel, tb=tb)
    return pl.pallas_call(
        kern,
        grid=(batch, nt),
        in_specs=[pl.BlockSpec((tb, PROJ_W), lambda b, t: (rb0 + b * nt + t, 0)),
                  pl.BlockSpec((1, 1, GROUP_W), smap),
                  pl.BlockSpec((1, 3, GROUP_W), smap),
                  pl.BlockSpec((1, 2, GROUP_W), smap),
                  vec,
                  pl.BlockSpec(ws.shape, cmap3),
                  pl.BlockSpec((tb, GROUP_W), cmap2),
                  pl.BlockSpec((4, GROUP_W), cmap2), vec,
                  wsq, vec, wsq, vec, vec, vec, vec,
                  pl.BlockSpec((3, GROUP_W), cmap2)],
        out_specs=[yspec, yspec, yspec, yspec, yspec,
                   pl.BlockSpec((1, 1, GROUP_W), smap),
                   pl.BlockSpec((1, 3, GROUP_W), smap),
                   pl.BlockSpec((1, 2, GROUP_W), smap),
                   pl.BlockSpec((1, tb, GROUP_W), smap)],
        out_shape=[y_sds, y_sds, y_sds, y_sds, y_sds,
                   jax.ShapeDtypeStruct((batch, 1, GROUP_W), F32),
                   jax.ShapeDtypeStruct((batch, 3, GROUP_W), F32),
                   jax.ShapeDtypeStruct((batch, 2, GROUP_W), F32),
                   jax.ShapeDtypeStruct((batch, tb, GROUP_W), F32)],
        scratch_shapes=[pltpu.VMEM((tb + 8, GROUP_W), F32),
                        pltpu.VMEM((tb + 8, GROUP_W), F32),
                        pltpu.VMEM((1, GROUP_W), F32)],
        compiler_params=_cparams(("arbitrary", "arbitrary")),
        name="mixer_pre",
    )(p_all, h0, lbuf, sbuf, wts["gng"], ws, wts["bsf"], wts["cw"], wts["cb"],
      wts["wa"], wts["ba"], wts["wx"], wts["bx"], wts["lam"], wts["qg"], wts["kg"], wts["sw"])


def _sb_setup(q, bias_ref, rows_per_head):
    nr = N_HEADS * rows_per_head
    lane_head = _lane_head()
    qs = jnp.concatenate([jnp.where(lane_head == hd, q, 0.0) for hd in range(N_HEADS)],
                         axis=0).astype(BF16)
    row_head = lax.broadcasted_iota(jnp.int32, (nr, LANES), 0) >> int(math.log2(rows_per_head))
    bias = jnp.full((nr, LANES), bias_ref[N_HEADS - 1], F32)
    for hd in range(N_HEADS - 2, -1, -1):
        bias = jnp.where(row_head == hd, bias_ref[hd], bias)
    jj = lax.broadcasted_iota(jnp.int32, (LANES, 2 * LANES), 0)
    ss = lax.broadcasted_iota(jnp.int32, (LANES, 2 * LANES), 1)
    cum = jnp.where((jj > ss) | (ss >= LANES), 1.0, 0.0).astype(BF16)
    return qs, bias, jnp.concatenate([cum, cum], axis=0)


_NT = (((1,), (1,)), ((), ()))


def _sb_tile(qs, bias, cum, k, v, acc, c, mask):
    z = lax.dot_general(qs, k.astype(BF16), _NT, preferred_element_type=F32) + bias
    lk = -(jnp.maximum(z, 0.0) + jnp.log1p(jnp.exp(-jnp.abs(z))))
    lkm = lk if mask is None else jnp.where(mask, lk, 0.0)
    r = _split_dot(lkm, cum)
    w = jnp.exp(z + lk + r[:, :LANES] + c)
    if mask is not None:
        w = jnp.where(mask, w, 0.0)
    acc = acc + jnp.dot(w.astype(BF16), v.astype(BF16), preferred_element_type=F32)
    return acc, c + r[:, LANES:]


def _sb_tile_ref(qs, bias, cum, k, v, acc_ref, c_ref, mask):
    acc, c = _sb_tile(qs, bias, cum, k, v, acc_ref[...], c_ref[...], mask)
    acc_ref[...] = acc
    c_ref[...] = c


def _sb_finish(acc, o_ref, rows_per_head):
    lane_head = _lane_head()
    out = jnp.zeros((rows_per_head, GROUP_W), F32)
    for hd in range(N_HEADS):
        out = out + jnp.where(lane_head == hd,
                              acc[hd * rows_per_head:(hd + 1) * rows_per_head, :], 0.0)
    o_ref[...] = out


def _sb_prompt_kernel(bias_ref, q_ref, k_ref, v_ref, o_ref, acc_ref, c_ref):
    i = pl.program_id(1)
    nr = N_HEADS * CHUNK
    qs, bias, cum = _sb_setup(q_ref[...] * (HEAD_DIM ** -0.5), bias_ref, CHUNK)
    acc_ref[...] = jnp.zeros_like(acc_ref)
    c_ref[...] = jnp.zeros_like(c_ref)
    tq = lax.broadcasted_iota(jnp.int32, (nr, LANES), 0) & (CHUNK - 1)
    ts = lax.broadcasted_iota(jnp.int32, (nr, LANES), 1)
    d0 = pl.multiple_of(i * CHUNK, CHUNK)
    _sb_tile_ref(qs, bias, cum, k_ref[pl.ds(d0, CHUNK), :], v_ref[pl.ds(d0, CHUNK), :],
                 acc_ref, c_ref, ts < tq)

    odd = i & 1

    @pl.when(odd == 1)
    def _():
        k0 = pl.multiple_of((i - 1) * CHUNK, CHUNK)
        _sb_tile_ref(qs, bias, cum, k_ref[pl.ds(k0, CHUNK), :], v_ref[pl.ds(k0, CHUNK), :],
                     acc_ref, c_ref, None)

    def body(j, carry):
        ka = pl.multiple_of((i - odd - 1 - 2 * j) * CHUNK, CHUNK)
        kb = pl.multiple_of((i - odd - 2 - 2 * j) * CHUNK, CHUNK)
        acc, c = _sb_tile(qs, bias, cum, k_ref[pl.ds(ka, CHUNK), :], v_ref[pl.ds(ka, CHUNK), :],
                          acc_ref[...], c_ref[...], None)
        acc, c = _sb_tile(qs, bias, cum, k_ref[pl.ds(kb, CHUNK), :], v_ref[pl.ds(kb, CHUNK), :],
                          acc, c, None)
        acc_ref[...] = acc
        c_ref[...] = c
        return carry

    lax.fori_loop(0, i >> 1, body, 0)
    _sb_finish(acc_ref[...], o_ref, CHUNK)


def _sb_prompt(sb_bias, q_p, k_p, p_all, batch, seq):
    nq = seq // CHUNK
    return pl.pallas_call(
        _sb_prompt_kernel,
        grid=(batch, nq),
        in_specs=[pl.BlockSpec(memory_space=pltpu.SMEM),
                  pl.BlockSpec((CHUNK, GROUP_W), lambda b, i: (b * nq + i, 0)),
                  pl.BlockSpec((seq, GROUP_W), lambda b, i: (b, 0)),
                  pl.BlockSpec((seq, GROUP_W), lambda b, i: (b, 6))],
        out_specs=pl.BlockSpec((CHUNK, GROUP_W), lambda b, i: (b * nq + i, 0)),
        out_shape=jax.ShapeDtypeStruct((batch * seq, GROUP_W), F32),
        scratch_shapes=[pltpu.VMEM((N_HEADS * CHUNK, GROUP_W), F32),
                        pltpu.VMEM((N_HEADS * CHUNK, LANES), F32)],
        compiler_params=_cparams(("parallel", "arbitrary")),
        name="sb_prompt",
    )(sb_bias, q_p, k_p, p_all)


def _sb_sample_kernel(pt_ref, bias_ref, q_ref, kn_ref, vn_ref, *rest, tq, n_pages):
    kp_refs = rest[:n_pages]
    vp_refs = rest[n_pages:2 * n_pages]
    o_ref, kpad_ref, vpad_ref = rest[2 * n_pages:]
    nr = N_HEADS * tq
    qs, bias, cum = _sb_setup(q_ref[...] * (HEAD_DIM ** -0.5), bias_ref, tq)
    kpad_ref[...] = jnp.zeros_like(kpad_ref)
    vpad_ref[...] = jnp.zeros_like(vpad_ref)
    kpad_ref[0:tq, :] = kn_ref[...]
    vpad_ref[0:tq, :] = vn_ref[...]
    trow = lax.broadcasted_iota(jnp.int32, (nr, LANES), 0) & (tq - 1)
    ts = lax.broadcasted_iota(jnp.int32, (nr, LANES), 1)
    mask = ts < trow
    zs = [lax.dot_general(qs, kpad_ref[...].astype(BF16), _NT, preferred_element_type=F32) + bias]
    for j in range(n_pages - 1, -1, -1):
        zs.append(jnp.dot(qs, kp_refs[j][0].astype(BF16), preferred_element_type=F32) + bias)
    lks = [-(jnp.maximum(z, 0.0) + jnp.log1p(jnp.exp(-jnp.abs(z)))) for z in zs]
    lkms = [jnp.where(mask, lks[0], 0.0)] + lks[1:]
    r_all = _split_dot(jnp.concatenate(lkms, axis=0), cum)
    acc = jnp.zeros((nr, GROUP_W), F32)
    c = jnp.zeros((nr, LANES), F32)
    for b in range(n_pages + 1):
        r = r_all[b * nr:(b + 1) * nr, :]
        w = jnp.exp(zs[b] + lks[b] + r[:, :LANES] + c)
        if b == 0:
            w = jnp.where(mask, w, 0.0)
            acc = acc + jnp.dot(w.astype(BF16), vpad_ref[...].astype(BF16),
                                preferred_element_type=F32)
        else:
            acc = acc + lax.dot_general(w.astype(BF16), vp_refs[n_pages - b][0].astype(BF16), _NT,
                                        preferred_element_type=F32)
        c = c + r[:, LANES:]
    _sb_finish(acc, o_ref, tq)


def _sb_sample(page_rows, sb_bias, q_s, k_s, p_all, row0, pool_k, pool_v, batch, tq):
    n_pages = page_rows.shape[1]
    page_rows = page_rows.reshape(-1)
    rb0 = row0 // tq

    def page_spec(j):
        return pl.BlockSpec((1, GROUP_W, PAGE), lambda b, pt: (pt[b * n_pages + j], 0, 0))

    kern = functools.partial(_sb_sample_kernel, tq=tq, n_pages=n_pages)
    grid_spec = pltpu.PrefetchScalarGridSpec(
        num_scalar_prefetch=1,
        grid=(batch,),
        in_specs=([pl.BlockSpec(memory_space=pltpu.SMEM),
                   pl.BlockSpec((tq, GROUP_W), lambda b, pt: (b, 0)),
                   pl.BlockSpec((tq, GROUP_W), lambda b, pt: (b, 0)),
                   pl.BlockSpec((tq, GROUP_W), lambda b, pt: (rb0 + b, 6))]
                  + [page_spec(j) for j in range(n_pages)]
                  + [page_spec(j) for j in range(n_pages)]),
        out_specs=pl.BlockSpec((tq, GROUP_W), lambda b, pt: (b, 0)),
        scratch_shapes=[pltpu.VMEM((PAGE, GROUP_W), F32),
                        pltpu.VMEM((PAGE, GROUP_W), F32)])
    return pl.pallas_call(
        kern,
        grid_spec=grid_spec,
        out_shape=jax.ShapeDtypeStruct((batch * tq, GROUP_W), F32),
        compiler_params=_cparams(("parallel",)),
        name="sb_sample",
    )(page_rows, sb_bias, q_s, k_s, p_all, *([pool_k] * n_pages), *([pool_v] * n_pages))


def _mem_attn_kernel(q_ref, k_ref, v_ref, o_ref):
    scale = MEM_HEAD_DIM ** -0.5
    for hd in range(MEM_HEADS):
        sl = slice(hd * MEM_HEAD_DIM, (hd + 1) * MEM_HEAD_DIM)
        qh = q_ref[:, sl].astype(BF16)
        kh = k_ref[0, :, hd, :].astype(BF16)
        s = lax.dot_general(qh, kh, (((1,), (1,)), ((), ())), preferred_element_type=F32) * scale
        s = s - jnp.max(s, axis=-1, keepdims=True)
        e = jnp.exp(s)
        p = e / jnp.sum(e, axis=-1, keepdims=True)
        o_ref[:, sl] = jnp.dot(p.astype(BF16), v_ref[0, :, hd, :].astype(BF16),
                               preferred_element_type=F32)


def _mem_attn(q_all, row0, batch, seq, tq, mem_k, mem_v, kv0):
    nt = seq // tq
    rb0 = row0 // tq
    kvspec = pl.BlockSpec((1, N_MEM, MEM_HEADS, MEM_HEAD_DIM), lambda b, t: (kv0 + b, 0, 0, 0))
    return pl.pallas_call(
        _mem_attn_kernel,
        grid=(batch, nt),
        in_specs=[pl.BlockSpec((tq, MEM_W), lambda b, t: (rb0 + b * nt + t, 0)), kvspec, kvspec],
        out_specs=pl.BlockSpec((tq, MEM_W), lambda b, t: (b * nt + t, 0)),
        out_shape=jax.ShapeDtypeStruct((batch * seq, MEM_W), F32),
        compiler_params=_cparams(("parallel", "arbitrary")),
        name="mem_attn",
    )(q_all, mem_k, mem_v)


def _router_kernel(x_ref, g_ref, w_ref, b_ref, h_ref, idx_ref, gate_ref, rank_ref, cnt_ref,
                   run_ref):
    i = pl.program_id(0)

    @pl.when(i == 0)
    def _():
        run_ref[...] = jnp.zeros_like(run_ref)

    hb = _rms(x_ref[...], g_ref[...]).astype(BF16)
    h_ref[...] = hb.astype(F32)
    logits = jnp.dot(hb, w_ref[...], preferred_element_type=F32) + b_ref[...]
    tm = logits.shape[0]
    lane = lax.broadcasted_iota(jnp.int32, logits.shape, 1).astype(F32)
    vals = jnp.where(lane < N_EXPERTS, logits, -jnp.inf)
    idx_out = jnp.zeros(logits.shape, F32)
    e_out = jnp.zeros(logits.shape, F32)
    picked = jnp.zeros(logits.shape, F32)
    sels = []
    m0 = None
    denom = None
    for kk in range(TOP_K):
        m = jnp.max(vals, axis=-1, keepdims=True)
        idx = jnp.min(jnp.where(vals == m, lane, float(LANES)), axis=-1, keepdims=True)
        sel = lane == idx
        sels.append(sel)
        picked = jnp.where(sel, 1.0, picked)
        vals = jnp.where(sel, -jnp.inf, vals)
        if kk == 0:
            m0 = m
        e = jnp.exp(m - m0)
        denom = e if kk == 0 else denom + e
        idx_out = jnp.where(lane == kk, idx, idx_out)
        e_out = jnp.where(lane == kk, e, e_out)
    idx_ref[...] = idx_out.astype(jnp.int32)
    gate_ref[...] = e_out / denom

    row = lax.broadcasted_iota(jnp.int32, (tm, tm), 0)
    col = lax.broadcasted_iota(jnp.int32, (tm, tm), 1)
    before = jnp.where(row > col, 1.0, 0.0).astype(BF16)
    prefix = jnp.dot(before, picked.astype(BF16), preferred_element_type=F32) + run_ref[...]
    rank_out = jnp.zeros(logits.shape, F32)
    for kk in range(TOP_K):
        rk = jnp.sum(jnp.where(sels[kk], prefix, 0.0), axis=-1, keepdims=True)
        rank_out = jnp.where(lane == kk, rk, rank_out)
    rank_ref[...] = rank_out.astype(jnp.int32)
    total = run_ref[...] + jnp.sum(picked, axis=0, keepdims=True)
    run_ref[...] = total
    cnt_ref[...] = total.astype(jnp.int32)


def _router(x, gain, w_bf16, bias):
    m = x.shape[0]
    rspec = pl.BlockSpec((ROW_TILE, LANES), lambda i: (i, 0))
    return pl.pallas_call(
        _router_kernel,
        grid=(m // ROW_TILE,),
        in_specs=[pl.BlockSpec((ROW_TILE, D_MODEL), lambda i: (i, 0)),
                  pl.BlockSpec((1, D_MODEL), lambda i: (0, 0)),
                  pl.BlockSpec((D_MODEL, LANES), lambda i: (0, 0)),
                  pl.BlockSpec((1, LANES), lambda i: (0, 0))],
        out_specs=[pl.BlockSpec((ROW_TILE, D_MODEL), lambda i: (i, 0)), rspec, rspec, rspec,
                   pl.BlockSpec((1, LANES), lambda i: (0, 0))],
        out_shape=[jax.ShapeDtypeStruct((m, D_MODEL), F32),
                   jax.ShapeDtypeStruct((m, LANES), jnp.int32),
                   jax.ShapeDtypeStruct((m, LANES), F32),
                   jax.ShapeDtypeStruct((m, LANES), jnp.int32),
                   jax.ShapeDtypeStruct((1, LANES), jnp.int32)],
        scratch_shapes=[pltpu.VMEM((1, LANES), F32)],
        compiler_params=_cparams(("arbitrary",)),
        name="router",
    )(x, gain, w_bf16, bias)


def _combine_kernel(x_ref, g_ref, y0_ref, y1_ref, y2_ref, y3_ref, o_ref):
    g = g_ref[...]
    y = ((y0_ref[...] * g[:, 0:1] + y1_ref[...] * g[:, 1:2])
         + (y2_ref[...] * g[:, 2:3] + y3_ref[...] * g[:, 3:4]))
    o_ref[...] = x_ref[...] + y


def _combine(x, gate_pad, yk):
    m, d = x.shape
    tm = 256
    nt = m // tm
    yspecs = [pl.BlockSpec((tm, d), lambda i, kk=kk: (kk * nt + i, 0)) for kk in range(TOP_K)]
    return pl.pallas_call(
        _combine_kernel,
        grid=(nt,),
        in_specs=[pl.BlockSpec((tm, d), lambda i: (i, 0)),
                  pl.BlockSpec((tm, LANES), lambda i: (i, 0))] + yspecs,
        out_specs=pl.BlockSpec((tm, d), lambda i: (i, 0)),
        out_shape=jax.ShapeDtypeStruct((m, d), F32),
        compiler_params=_cparams(("parallel",)),
        name="combine",
    )(x, gate_pad, yk, yk, yk, yk)


def _expert_kernel(be_ref, xoff_ref, nb_ref, xs_hbm, wi_ref, bi_ref, wo_ref, bo_ref, o_ref,
                   xbuf_ref, wib_ref, wob_ref, sem_ref):
    i = pl.program_id(0)
    nb = nb_ref[0]
    cur = i % 2

    def window_copy(step, buf):
        off = pl.multiple_of(xoff_ref[step], 8)
        return pltpu.make_async_copy(xs_hbm.at[pl.ds(off, MOE_ROWS), :], xbuf_ref.at[buf],
                                     sem_ref.at[buf])

    @pl.when(jnp.logical_and(i == 0, nb > 0))
    def _():
        window_copy(0, 0).start()

    @pl.when(i + 1 < nb)
    def _():
        window_copy(i + 1, 1 - cur).start()

    @pl.when(i < nb)
    def _():
        new_expert = jnp.logical_or(i == 0, be_ref[i] != be_ref[jnp.maximum(i - 1, 0)])

        @pl.when(new_expert)
        def _():
            rows = 64

            def cast_rows(r, carry):
                r0 = pl.multiple_of(r * rows, rows)
                wib_ref[pl.ds(r0, rows), :] = wi_ref[0, pl.ds(r0, rows), :].astype(BF16)
                wob_ref[pl.ds(r0, rows), :] = wo_ref[0, pl.ds(r0, rows), :].astype(BF16)
                return carry
            lax.fori_loop(0, D_MODEL // rows, cast_rows, 0)

        window_copy(i, cur).wait()
        x = xbuf_ref[cur].astype(BF16)
        acc = jnp.zeros(o_ref.shape, F32)
        for c0 in range(0, D_EXPERT, MOE_HCHUNK):
            g = jnp.dot(x, wib_ref[:, c0:c0 + MOE_HCHUNK], preferred_element_type=F32)
            g = g + bi_ref[0, :, c0:c0 + MOE_HCHUNK]
            u = jnp.dot(x, wib_ref[:, D_EXPERT + c0:D_EXPERT + c0 + MOE_HCHUNK],
                        preferred_element_type=F32)
            u = u + bi_ref[0, :, D_EXPERT + c0:D_EXPERT + c0 + MOE_HCHUNK]
            g = jnp.minimum(g, SWIGLU_LIMIT)
            u = jnp.clip(u, -SWIGLU_LIMIT, SWIGLU_LIMIT)
            act = g * jax.nn.sigmoid(SWIGLU_ALPHA * g) * (u + 1.0)
            acc = acc + jnp.dot(act.astype(BF16), wob_ref[c0:c0 + MOE_HCHUNK, :],
                                preferred_element_type=F32)
        o_ref[...] = acc + bo_ref[0]

    @pl.when(i >= nb)
    def _():
        o_ref[...] = jnp.zeros_like(o_ref)


def _experts(block_w, xoff, n_real, xs, wi, bi, wo, bo):
    n_blocks = block_w.shape[0]
    wmap = lambda i, bw, xo, nb: (bw[i], 0, 0)
    grid_spec = pltpu.PrefetchScalarGridSpec(
        num_scalar_prefetch=3,
        grid=(n_blocks,),
        in_specs=[pl.BlockSpec(memory_space=pl.ANY),
                  pl.BlockSpec((1, D_MODEL, 2 * D_EXPERT), wmap),
                  pl.BlockSpec((1, 1, 2 * D_EXPERT), wmap),
                  pl.BlockSpec((1, D_EXPERT, D_MODEL), wmap),
                  pl.BlockSpec((1, 1, D_MODEL), wmap)],
        out_specs=pl.BlockSpec((MOE_ROWS, D_MODEL), lambda i, bw, xo, nb: (i, 0)),
        scratch_shapes=[pltpu.VMEM((2, MOE_ROWS, D_MODEL), F32),
                        pltpu.VMEM((D_MODEL, 2 * D_EXPERT), BF16),
                        pltpu.VMEM((D_EXPERT, D_MODEL), BF16),
                        pltpu.SemaphoreType.DMA((2,))])
    return pl.pallas_call(
        _expert_kernel,
        grid_spec=grid_spec,
        out_shape=jax.ShapeDtypeStruct((n_blocks * MOE_ROWS, D_MODEL), F32),
        compiler_params=_cparams(("arbitrary",), VMEM_EXPERTS),
        name="experts",
    )(block_w, xoff, n_real, xs, wi, bi, wo, bo)


def _lookup(table, idx):
    experts = jnp.arange(N_EXPERTS, dtype=jnp.int32)
    return jnp.sum(jnp.where(idx[:, None] == experts[None, :], table[None, :], 0), axis=1)


def _moe(x, gain, w_router, rbias, wi, bi, wo, bo, layer):
    n = x.shape[0]
    h, idx_pad, gate_pad, rank_pad, cnt_pad = _router(x, gain, w_router, rbias)
    n_assign = n * TOP_K
    flat_e = idx_pad[:, :TOP_K].reshape(-1)
    flat_rank = rank_pad[:, :TOP_K].reshape(-1)
    counts = cnt_pad[0, :N_EXPERTS]
    starts = jnp.cumsum(counts) - counts
    padded = (counts + MOE_ROWS - 1) // MOE_ROWS * MOE_ROWS
    pad_ends = jnp.cumsum(padded)
    pad_starts = pad_ends - padded
    n_blocks = -(-n_assign // MOE_ROWS) + N_EXPERTS
    block_start = jnp.arange(n_blocks, dtype=jnp.int32) * MOE_ROWS
    block_e = jnp.minimum(jnp.sum(pad_ends[None, :] <= block_start[:, None], axis=1),
                          N_EXPERTS - 1).astype(jnp.int32)
    n_real = (pad_ends[-1] // MOE_ROWS).astype(jnp.int32).reshape(1)
    seg = (counts + 7) // 8 * 8
    seg_ends = jnp.cumsum(seg)
    seg_starts = seg_ends - seg
    n_in = n_assign + 8 * N_EXPERTS + MOE_ROWS
    row = jnp.arange(n_in, dtype=jnp.int32)
    shift = seg_starts - starts
    dshift = shift - jnp.concatenate([jnp.zeros((1,), jnp.int32), shift[:-1]])
    shift_row = jnp.sum(jnp.where(row[:, None] >= seg_starts[None, :], dshift[None, :], 0), axis=1)
    in_gap = jnp.any((row[:, None] >= (seg_starts + counts)[None, :])
                     & (row[:, None] < seg_ends[None, :]), axis=1)
    valid = jnp.logical_and(jnp.logical_not(in_gap), row < seg_ends[-1])
    tok_bits = max(n - 1, 1).bit_length()
    packed = (flat_e << tok_bits) | (jnp.arange(n_assign, dtype=jnp.int32) // TOP_K)
    sorted_tok = lax.sort(packed) & ((1 << tok_bits) - 1)
    row_tok = jnp.where(valid, sorted_tok[jnp.clip(row - shift_row, 0, n_assign - 1)], 0)
    xs = h[row_tok]
    xoff = (_lookup(seg_starts, block_e) + block_start
            - _lookup(pad_starts, block_e)).astype(jnp.int32)
    yb = _experts(block_e + layer * N_EXPERTS, xoff, n_real, xs, wi, bi, wo, bo)
    dest = (_lookup(pad_starts, flat_e) + flat_rank).reshape(n, TOP_K)
    yk = yb[dest.T.reshape(-1)]
    return _combine(x, gate_pad, yk)


def _block_diag(w):
    hh, d, _ = w.shape
    eye = jnp.eye(hh, dtype=w.dtype)
    return jnp.einsum('hij,hg->higj', w, eye).reshape(hh * d, hh * d)


def kernel(x_prompt, x_sample, cache_sb_k, cache_sb_v, page_table, state_lru_h, state_lru_conv, state_sconv, cache_mem_k, cache_mem_v, mem_prompt, ln_mix, w_in, gmlp_norm, gmlp_ws, gmlp_bs, lru_conv_w, lru_conv_b, lru_wa, lru_ba, lru_wx, lru_bx, lru_lambda, sb_q_norm, sb_k_norm, sb_bias, sconv_w, out_norm, w_out, ln_mem, ln_mem_kv, mem_wq, mem_wk, mem_wv, mem_q_norm, mem_k_norm, mem_wo, ln_moe, router_w, router_b, moe_w_in, moe_b_in, moe_w_out, moe_b_out):
    depth = w_in.shape[0]
    bp, sp, _ = x_prompt.shape
    bs, ss, _ = x_sample.shape
    n_p = bp * sp
    n_s = bs * ss
    n_pool = cache_sb_k.shape[1]

    x = jnp.concatenate([x_prompt.reshape(n_p, D_MODEL), x_sample.reshape(n_s, D_MODEL)], axis=0)
    mem2d = mem_prompt.reshape(bp * N_MEM, D_MODEL)
    pool_k = jnp.transpose(cache_sb_k, (0, 1, 3, 4, 2)).reshape(depth * n_pool, GROUP_W, PAGE)
    pool_v = jnp.transpose(cache_sb_v, (0, 1, 3, 4, 2)).reshape(depth * n_pool, GROUP_W, PAGE)
    cmem_k = cache_mem_k.reshape(depth * bs, N_MEM, MEM_HEADS, MEM_HEAD_DIM)
    cmem_v = cache_mem_v.reshape(depth * bs, N_MEM, MEM_HEADS, MEM_HEAD_DIM)

    w_in_b = w_in.astype(BF16)
    w_out_b = w_out.astype(BF16)
    mem_wq_b = mem_wq.astype(BF16)
    mem_wkv_b = jnp.concatenate([mem_wk, mem_wv], axis=2).astype(BF16)
    mem_wo_b = mem_wo.astype(BF16)
    moe_wi = moe_w_in.reshape(depth * N_EXPERTS, D_MODEL, 2 * D_EXPERT)
    moe_wo = moe_w_out.reshape(depth * N_EXPERTS, D_EXPERT, D_MODEL)
    moe_bi = moe_b_in.reshape(depth * N_EXPERTS, 1, 2 * D_EXPERT)
    moe_bo = moe_b_out.reshape(depth * N_EXPERTS, 1, D_MODEL)
    rw_b = jnp.pad(router_w, ((0, 0), (0, 0), (0, LANES - N_EXPERTS))).astype(BF16)
    rb_pad = jnp.pad(router_b, ((0, 0), (0, LANES - N_EXPERTS))).reshape(depth, 1, LANES)

    zeros_h = jnp.zeros((bp, 1, GROUP_W), F32)
    zeros_lb = jnp.zeros((bp, 3, GROUP_W), F32)
    zeros_sb = jnp.zeros((bp, 2, GROUP_W), F32)

    outs = [[] for _ in range(14)]
    for l in range(depth):
        mkv = _norm_matmul(mem2d, ln_mem_kv[l], mem_wkv_b[l],
                           head_gain=jnp.concatenate([jnp.tile(mem_k_norm[l], MEM_HEADS),
                                                      jnp.ones((MEM_W,), F32)]).reshape(1, 2 * MEM_W),
                           norm_cols=MEM_W, head_dim=MEM_HEAD_DIM)
        mk_p = mkv[:, :MEM_W].reshape(bp, N_MEM, MEM_HEADS, MEM_HEAD_DIM)
        mv_p = mkv[:, MEM_W:].reshape(bp, N_MEM, MEM_HEADS, MEM_HEAD_DIM)

        p_all = _norm_matmul(x, ln_mix[l], w_in_b[l])
        common = dict(
            gng=gmlp_norm[l].reshape(1, GROUP_W),
            cw=lru_conv_w[l], cb=lru_conv_b[l].reshape(1, GROUP_W),
            wa=_block_diag(lru_wa[l]).astype(BF16), ba=lru_ba[l].reshape(1, GROUP_W),
            wx=_block_diag(lru_wx[l]).astype(BF16), bx=lru_bx[l].reshape(1, GROUP_W),
            lam=lru_lambda[l].reshape(1, GROUP_W),
            qg=jnp.tile(sb_q_norm[l], N_HEADS).reshape(1, GROUP_W),
            kg=jnp.tile(sb_k_norm[l], N_HEADS).reshape(1, GROUP_W),
            sw=sconv_w[l])
        lp = min(CHUNK, sp)
        ls = min(CHUNK, ss)
        wts_p = dict(common, ws=gmlp_ws[l][:, :lp, :lp],
                     bsf=jnp.repeat(gmlp_bs[l][:, :lp].T, HEAD_DIM, axis=1))
        ws_exp = jnp.repeat(jnp.transpose(gmlp_ws[l][:, :ls, :ls], (2, 1, 0)), HEAD_DIM, axis=2)
        wts_s = dict(common, ws=ws_exp, bsf=jnp.repeat(gmlp_bs[l][:, :ls].T, HEAD_DIM, axis=1))
        (ya_p, yb_p, yd_p, q_p, k_p, hl_p, lb_p, sbuf_p, gv_p) = _mixer_pre(
            p_all, 0, bp, sp, lp, zeros_h, zeros_lb, zeros_sb, wts_p)
        (ya_s, yb_s, yd_s, q_s, k_s, hl_s, lb_s, sbuf_s, gv_s) = _mixer_pre(
            p_all, n_p, bs, ss, ls, state_lru_h[l].reshape(bs, 1, GROUP_W), state_lru_conv[l],
            state_sconv[l], wts_s)
        yc_p = _sb_prompt(sb_bias[l], q_p, k_p, p_all, bp, sp)
        yc_s = _sb_sample(page_table + l * n_pool, sb_bias[l], q_s, k_s, p_all, n_p,
                          pool_k, pool_v, bs, ss)
        cat = lambda a, b: jnp.concatenate([a, b], axis=0)
        x = _merge(cat(ya_p, ya_s), cat(yb_p, yb_s), cat(yc_p, yc_s), cat(yd_p, yd_s),
                   out_norm[l], w_out_b[l], x)

        q_mem = _norm_matmul(x, ln_mem[l], mem_wq_b[l],
                             head_gain=jnp.tile(mem_q_norm[l], MEM_HEADS).reshape(1, MEM_W),
                             norm_cols=MEM_W, head_dim=MEM_HEAD_DIM)
        a_p = _mem_attn(q_mem, 0, bp, sp, 512, mk_p, mv_p, 0)
        a_s = _mem_attn(q_mem, n_p, bs, ss, ss, cmem_k, cmem_v, l * bs)
        x = _matmul_res(cat(a_p, a_s), mem_wo_b[l], x)

        x = _moe(x, ln_moe[l].reshape(1, D_MODEL), rw_b[l], rb_pad[l],
                 moe_wi, moe_bi, moe_wo, moe_bo, l)

        v_all = p_all[:, 1536:1792]
        outs[0].append(k_p.reshape(bp, sp, N_HEADS, HEAD_DIM))
        outs[1].append(v_all[:n_p].reshape(bp, sp, N_HEADS, HEAD_DIM))
        outs[2].append(k_s.reshape(bs, ss, N_HEADS, HEAD_DIM))
        outs[3].append(v_all[n_p:].reshape(bs, ss, N_HEADS, HEAD_DIM))
        outs[4].append(hl_p.reshape(bp, GROUP_W))
        outs[5].append(hl_s.reshape(bs, GROUP_W))
        outs[6].append(lb_p)
        outs[7].append(lb_s)
        outs[8].append(sbuf_p)
        outs[9].append(sbuf_s)
        outs[10].append(gv_p.reshape(bp, lp, N_HEADS, HEAD_DIM))
        outs[11].append(gv_s.reshape(bs, ls, N_HEADS, HEAD_DIM))
        outs[12].append(mk_p)
        outs[13].append(mv_p)

    y_prompt = x[:n_p].reshape(bp, sp, D_MODEL)
    y_sample = x[n_p:].reshape(bs, ss, D_MODEL)
    return (y_prompt, y_sample) + tuple(jnp.stack(o) for o in outs)
```
